```python
import math, functools
import jax, jax.numpy as jnp
from jax import lax
import numpy as np

D_MODEL = 2048
BATCH = 2
SEQ = 4096
DEPTH = 4

CHUNK = 64
QBLK = 128
N_MIXERS = 3
EPS = 1e-6
MLA_HEADS = 16
MLA_Q_LORA = 512
MLA_KV_LORA = 512
MLA_NOPE = 128
MLA_ROPE = 64
MLA_V = 128
ROPE_THETA = 10000.0
CONV_WIDTH = 3
SSM_D_INNER = 2 * D_MODEL
SSM_HEAD_DIM = 64
SSM_HEADS = SSM_D_INNER // SSM_HEAD_DIM
SSM_GROUPS = 8
SSM_STATE = 128
SSM_CONV = 4
SSM_CONV_DIM = SSM_D_INNER + 2 * SSM_GROUPS * SSM_STATE
SSM_IN_DIM = 2 * SSM_D_INNER + 2 * SSM_GROUPS * SSM_STATE + SSM_HEADS
MLP_HIDDEN = 4 * D_MODEL
N_MLA_LAYERS = len(range(0, DEPTH, N_MIXERS))
N_CONV_LAYERS = len(range(1, DEPTH, N_MIXERS))
N_SSM_LAYERS = len(range(2, DEPTH, N_MIXERS))

kernel_name = "hybrid_mla_shortconv_ssd_trunk"


def rms(x):
    xf = x.astype(jnp.float32)
    return (xf * lax.rsqrt(jnp.mean(xf * xf, axis=-1, keepdims=True) + EPS)).astype(x.dtype)


def rmsnorm(x, g):
    return rms(x) * g


def causal_dwconv(u, w):
    k, ch = w.shape
    return lax.conv_general_dilated(
        u, w[:, None, :], window_strides=(1,), padding=[(k - 1, 0)],
        dimension_numbers=("NWC", "WIO", "NWC"), feature_group_count=ch)


def rope_tables(positions):
    inv_freq = ROPE_THETA ** (-jnp.arange(0, MLA_ROPE, 2, dtype=jnp.float32) / MLA_ROPE)
    ang = positions.astype(jnp.float32)[..., None] * inv_freq
    return jnp.cos(ang)[:, :, None, :], jnp.sin(ang)[:, :, None, :]


def apply_rope(t, cos, sin):
    t1, t2 = jnp.split(t, 2, axis=-1)
    return jnp.concatenate([t1 * cos - t2 * sin, t2 * cos + t1 * sin], axis=-1).astype(t.dtype)


def chunk_causal_attention(q, k, v):
    b, s, h, dqk = q.shape
    nblk = s // QBLK
    scale = 1.0 / math.sqrt(dqk)
    key_chunk = jnp.arange(s) // CHUNK
    q_blocks = q.reshape(b, nblk, QBLK, h, dqk).transpose(1, 0, 2, 3, 4)

    def attend(args):
        qb, bi = args
        q_chunk = (bi * QBLK + jnp.arange(QBLK)) // CHUNK
        mask = key_chunk[None, :] <= q_chunk[:, None]
        sc = jnp.einsum("bqhd,bkhd->bhqk", qb, k).astype(jnp.float32) * scale
        sc = jnp.where(mask[None, None], sc, -jnp.inf)
        p = jax.nn.softmax(sc, axis=-1).astype(v.dtype)
        return jnp.einsum("bhqk,bkhd->bqhd", p, v)

    out = lax.map(attend, (q_blocks, jnp.arange(nblk)))
    return out.transpose(1, 0, 2, 3, 4).reshape(b, s, h * v.shape[-1])


def mla_mixer(h, cos, sin, w_down, q_norm, w_uq, kv_norm, w_ukv, w_o):
    b, s, _ = h.shape
    down = h @ w_down
    cq = down[..., :MLA_Q_LORA]
    ckv = down[..., MLA_Q_LORA:MLA_Q_LORA + MLA_KV_LORA]
    k_rope = down[..., MLA_Q_LORA + MLA_KV_LORA:]
    q = (rmsnorm(cq, q_norm) @ w_uq).reshape(b, s, MLA_HEADS, MLA_NOPE + MLA_ROPE)
    kv = (rmsnorm(ckv, kv_norm) @ w_ukv).reshape(b, s, MLA_HEADS, MLA_NOPE + MLA_V)
    q = jnp.concatenate([q[..., :MLA_NOPE], apply_rope(q[..., MLA_NOPE:], cos, sin)], axis=-1)
    k_rope = apply_rope(k_rope[:, :, None, :], cos, sin)
    k = jnp.concatenate([kv[..., :MLA_NOPE],
                         jnp.broadcast_to(k_rope, (b, s, MLA_HEADS, MLA_ROPE))], axis=-1)
    v = kv[..., MLA_NOPE:]
    return chunk_causal_attention(q, k, v) @ w_o


def shortconv_mixer(h, w_in, conv_w, w_out):
    gb, gc, u = jnp.split(h @ w_in, 3, axis=-1)
    return (gb * causal_dwconv(gc * u, conv_w)) @ w_out


def ssd_scan(x, dt, a, bm, cm):
    b, s, nh, p = x.shape
    nc, L, G, hg = s // CHUNK, CHUNK, SSM_GROUPS, nh // SSM_GROUPS
    xf = x.astype(jnp.float32).reshape(b, nc, L, G, hg, p)
    dtc = dt.reshape(b, nc, L, G, hg)
    a_cum = jnp.cumsum(dtc * a.reshape(G, hg), axis=2)
    xdt = xf * dtc[..., None]
    bf = bm.astype(jnp.float32).reshape(b, nc, L, G, SSM_STATE)
    cf = cm.astype(jnp.float32).reshape(b, nc, L, G, SSM_STATE)
    idx = jnp.arange(L)
    causal = idx[:, None] >= idx[None, :]
    seg = a_cum[:, :, :, None] - a_cum[:, :, None, :]
    decay = jnp.exp(jnp.where(causal[:, :, None, None], seg, -jnp.inf))
    cb = jnp.einsum("bclgn,bcsgn->bclsg", cf, bf)
    y_diag = jnp.einsum("bclsg,bclsgh,bcsghp->bclghp", cb, decay, xdt)
    decay_to_end = jnp.exp(a_cum[:, :, -1:] - a_cum)
    states = jnp.einsum("bclgn,bclgh,bclghp->bcghpn", bf, decay_to_end, xdt)
    chunk_decay = jnp.exp(a_cum[:, :, -1])

    def step(carry, inp):
        dec, st = inp
        return carry * dec[..., None, None] + st, carry

    init = jnp.zeros((b, G, hg, p, SSM_STATE), jnp.float32)
    _, prev = lax.scan(step, init, (jnp.moveaxis(chunk_decay, 1, 0), jnp.moveaxis(states, 1, 0)))
    prev = jnp.moveaxis(prev, 0, 1)
    y_off = jnp.einsum("bclgn,bcghpn,bclgh->bclghp", cf, prev, jnp.exp(a_cum))
    return (y_diag + y_off).reshape(b, s, nh, p).astype(x.dtype)


def ssd_mixer(h, w_in, conv_w, conv_b, dt_bias, a_log, d_skip, norm_w, w_out):
    b, s, _ = h.shape
    zxbcdt = h @ w_in
    z = zxbcdt[..., :SSM_D_INNER]
    xbc = zxbcdt[..., SSM_D_INNER:SSM_D_INNER + SSM_CONV_DIM]
    dt = zxbcdt[..., SSM_D_INNER + SSM_CONV_DIM:]
    xbc = jax.nn.silu(causal_dwconv(xbc, conv_w) + conv_b)
    gn = SSM_GROUPS * SSM_STATE
    xs = xbc[..., :SSM_D_INNER].reshape(b, s, SSM_HEADS, SSM_HEAD_DIM)
    bm = xbc[..., SSM_D_INNER:SSM_D_INNER + gn].reshape(b, s, SSM_GROUPS, SSM_STATE)
    cm = xbc[..., SSM_D_INNER + gn:].reshape(b, s, SSM_GROUPS, SSM_STATE)
    dt = jax.nn.softplus(dt.astype(jnp.float32) + dt_bias.astype(jnp.float32))
    a = -jnp.exp(a_log.astype(jnp.float32))
    y = ssd_scan(xs, dt, a, bm, cm) + xs * d_skip[:, None]
    y = y.reshape(b, s, SSM_D_INNER) * jax.nn.silu(z)
    y = rms(y.reshape(b, s, SSM_GROUPS, SSM_D_INNER // SSM_GROUPS)).reshape(b, s, SSM_D_INNER) * norm_w
    return y @ w_out


def squared_relu_mlp(h, w_up, w_down):
    return jnp.square(jax.nn.relu(h @ w_up)) @ w_down


def sublayer(x, c_act, w_ada, b_ada, g_pre, g_post, fn):
    shift, scale, gate = jnp.split(c_act @ w_ada + b_ada, 3, axis=-1)
    h = rmsnorm(x, g_pre) * (1 + scale[:, None, :]) + shift[:, None, :]
    return x + gate[:, None, :] * rmsnorm(fn(h), g_post)


def setup_inputs(seed: int = 0) -> dict:
    key = jax.random.key(seed)
    ks = jax.random.split(key, 32)
    D = D_MODEL
    f32 = jnp.float32

    def nrm(k, shape, fan_in):
        return jax.random.normal(k, shape, f32) * fan_in ** -0.5

    def gain(k, shape):
        return 1.0 + 0.05 * jax.random.normal(k, shape, f32)

    x = jax.random.normal(ks[0], (BATCH, SEQ, D), f32)
    c = jax.random.normal(ks[1], (BATCH, D), f32)
    positions = (jax.random.randint(ks[2], (BATCH, 1), 0, 1024, jnp.int32)
                 + jnp.arange(SEQ, dtype=jnp.int32)[None, :])
    ada_w = nrm(ks[3], (DEPTH, 2, D, 3 * D), D)
    ada_b = 0.02 * jax.random.normal(ks[4], (DEPTH, 2, 3 * D), f32)
    norm_pre = gain(ks[5], (DEPTH, 2, D))
    norm_post = gain(ks[6], (DEPTH, 2, D))
    NA, NB, NC = N_MLA_LAYERS, N_CONV_LAYERS, N_SSM_LAYERS
    mla_w_down = nrm(ks[7], (NA, D, MLA_Q_LORA + MLA_KV_LORA + MLA_ROPE), D)
    mla_q_norm = gain(ks[8], (NA, MLA_Q_LORA))
    mla_w_uq = nrm(ks[9], (NA, MLA_Q_LORA, MLA_HEADS * (MLA_NOPE + MLA_ROPE)), MLA_Q_LORA)
    mla_kv_norm = gain(ks[10], (NA, MLA_KV_LORA))
    mla_w_ukv = nrm(ks[11], (NA, MLA_KV_LORA, MLA_HEADS * (MLA_NOPE + MLA_V)), MLA_KV_LORA)
    mla_w_o = nrm(ks[12], (NA, MLA_HEADS * MLA_V, D), MLA_HEADS * MLA_V)
    conv_w_in = nrm(ks[13], (NB, D, 3 * D), D)
    conv_w = nrm(ks[14], (NB, CONV_WIDTH, D), CONV_WIDTH)
    conv_w_out = nrm(ks[15], (NB, D, D), D)
    ssm_w_in = nrm(ks[16], (NC, D, SSM_IN_DIM), D)
    ssm_conv_w = nrm(ks[17], (NC, SSM_CONV, SSM_CONV_DIM), SSM_CONV)
    ssm_conv_b = 0.02 * jax.random.normal(ks[18], (NC, SSM_CONV_DIM), f32)
    dt0 = jnp.exp(jax.random.uniform(ks[19], (NC, SSM_HEADS), f32,
                                     minval=math.log(1e-3), maxval=math.log(1e-1)))
    ssm_dt_bias = dt0 + jnp.log(-jnp.expm1(-dt0))
    ssm_a_log = jnp.log(jax.random.uniform(ks[20], (NC, SSM_HEADS), f32, minval=1.0, maxval=16.0))
    ssm_d = 1.0 + 0.1 * jax.random.normal(ks[21], (NC, SSM_HEADS), f32)
    ssm_norm = gain(ks[22], (NC, SSM_D_INNER))
    ssm_w_out = nrm(ks[23], (NC, SSM_D_INNER, D), SSM_D_INNER)
    mlp_up = nrm(ks[24], (DEPTH, D, MLP_HIDDEN), D)
    mlp_down = nrm(ks[25], (DEPTH, MLP_HIDDEN, D), MLP_HIDDEN)
    return {"x": x, "c": c, "positions": positions, "ada_w": ada_w, "ada_b": ada_b,
            "norm_pre": norm_pre, "norm_post": norm_post,
            "mla_w_down": mla_w_down, "mla_q_norm": mla_q_norm, "mla_w_uq": mla_w_uq,
            "mla_kv_norm": mla_kv_norm, "mla_w_ukv": mla_w_ukv, "mla_w_o": mla_w_o,
            "conv_w_in": conv_w_in, "conv_w": conv_w, "conv_w_out": conv_w_out,
            "ssm_w_in": ssm_w_in, "ssm_conv_w": ssm_conv_w, "ssm_conv_b": ssm_conv_b,
            "ssm_dt_bias": ssm_dt_bias, "ssm_a_log": ssm_a_log, "ssm_d": ssm_d,
            "ssm_norm": ssm_norm, "ssm_w_out": ssm_w_out,
            "mlp_up": mlp_up, "mlp_down": mlp_down}


def reference(x, c, positions, ada_w, ada_b, norm_pre, norm_post,
              mla_w_down, mla_q_norm, mla_w_uq, mla_kv_norm, mla_w_ukv, mla_w_o,
              conv_w_in, conv_w, conv_w_out,
              ssm_w_in, ssm_conv_w, ssm_conv_b, ssm_dt_bias, ssm_a_log, ssm_d, ssm_norm, ssm_w_out,
              mlp_up, mlp_down):
    c_act = jax.nn.silu(c)
    cos, sin = rope_tables(positions)
    for i in range(DEPTH):
        kind, j = i % N_MIXERS, i // N_MIXERS
        if kind == 0:
            mixer = functools.partial(mla_mixer, cos=cos, sin=sin, w_down=mla_w_down[j],
                                      q_norm=mla_q_norm[j], w_uq=mla_w_uq[j], kv_norm=mla_kv_norm[j],
                                      w_ukv=mla_w_ukv[j], w_o=mla_w_o[j])
        elif kind == 1:
            mixer = functools.partial(shortconv_mixer, w_in=conv_w_in[j], conv_w=conv_w[j],
                                      w_out=conv_w_out[j])
        else:
            mixer = functools.partial(ssd_mixer, w_in=ssm_w_in[j], conv_w=ssm_conv_w[j],
                                      conv_b=ssm_conv_b[j], dt_bias=ssm_dt_bias[j], a_log=ssm_a_log[j],
                                      d_skip=ssm_d[j], norm_w=ssm_norm[j], w_out=ssm_w_out[j])
        x = sublayer(x, c_act, ada_w[i, 0], ada_b[i, 0], norm_pre[i, 0], norm_post[i, 0], mixer)
        mlp = functools.partial(squared_relu_mlp, w_up=mlp_up[i], w_down=mlp_down[i])
        x = sublayer(x, c_act, ada_w[i, 1], ada_b[i, 1], norm_pre[i, 1], norm_post[i, 1], mlp)
    return x
```

```python
import functools
import math

import jax
import jax.numpy as jnp
from jax import lax
from jax.experimental import pallas as pl
from jax.experimental.pallas import tpu as pltpu

F32 = jnp.float32
BF16 = jnp.bfloat16

EPS = 1e-6
CHUNK = 64
N_MIXERS = 3
MLA_HEADS = 16
MLA_LORA = 512
MLA_NOPE = 128
MLA_ROPE = 64
MLA_V = 128
ROPE_THETA = 10000.0
SSM_HEAD_DIM = 64
SSM_GROUPS = 8
SSM_STATE = 128

LANES = 128
SUBLANES = 8
V7X_VMEM_BYTES = 64 * 1024 * 1024
HEAD_PAD = 2 * LANES


def _vmem_limit(block_bytes, scratch_bytes=0, temp_bytes=0):
    need = 2 * sum(block_bytes) + scratch_bytes + temp_bytes + (4 << 20)
    return int(min(need, V7X_VMEM_BYTES - (6 << 20)))


def _nbytes(shape, dtype):
    return math.prod(shape) * jnp.dtype(dtype).itemsize


def _params(sem, limit):
    return pltpu.CompilerParams(dimension_semantics=sem, vmem_limit_bytes=limit)


def _rms(x):
    return x * lax.rsqrt(jnp.mean(x * x, axis=-1, keepdims=True) + EPS)


def _modulated_norm(x_ref, mod_ref, g_ref):
    h = _rms(x_ref[...]) * g_ref[...]
    return h * (1.0 + mod_ref[1:2, :]) + mod_ref[0:1, :]


def _silu(x):
    return x * (1.0 / (1.0 + jnp.exp(-x)))


def _ada_kernel(c_ref, w_ref, b_ref, o_ref):
    c_act = _silu(c_ref[...]).astype(BF16)
    y = jnp.dot(c_act, w_ref[...].astype(BF16), preferred_element_type=F32)
    o_ref[...] = y + b_ref[...]


def _ada_all(c, ada_w, ada_b):
    nsub = ada_w.shape[0] * ada_w.shape[1]
    bsz, d = c.shape
    n = ada_w.shape[-1]
    w = ada_w.reshape(nsub, d, n)
    b = ada_b.reshape(nsub, 1, n)
    tn = 1024
    blocks = [_nbytes((d, tn), F32), _nbytes((bsz, tn), F32)]
    return pl.pallas_call(
        _ada_kernel,
        out_shape=jax.ShapeDtypeStruct((nsub, bsz, n), F32),
        grid=(nsub, n // tn),
        in_specs=[
            pl.BlockSpec((bsz, d), lambda s, j: (0, 0)),
            pl.BlockSpec((None, d, tn), lambda s, j: (s, 0, j)),
            pl.BlockSpec((None, 1, tn), lambda s, j: (s, 0, j)),
        ],
        out_specs=pl.BlockSpec((None, bsz, tn), lambda s, j: (s, 0, j)),
        compiler_params=_params(("arbitrary", "arbitrary"), _vmem_limit(blocks, temp_bytes=_nbytes((d, tn), BF16))),
        name="ada_mod",
    )(c, w, b)


def _rope_kernel(pos_ref, freq_ref, cos_ref, sin_ref):
    ang = pos_ref[...] * freq_ref[...]
    lane = lax.broadcasted_iota(jnp.int32, ang.shape, 1)
    half = MLA_ROPE // 2
    cos_ref[...] = jnp.where(lane < MLA_ROPE, jnp.cos(ang), 0.0)
    s = jnp.sin(ang)
    sin_ref[...] = jnp.where(lane < half, -s, jnp.where(lane < MLA_ROPE, s, 0.0))


def _rope_tables(positions):
    t = positions.size
    half = MLA_ROPE // 2
    inv_freq = ROPE_THETA ** (-jnp.arange(0, MLA_ROPE, 2, dtype=F32) / MLA_ROPE)
    freq = jnp.concatenate([inv_freq, inv_freq, jnp.zeros((LANES - 2 * half,), F32)]).reshape(1, LANES)
    pos = positions.astype(F32).reshape(t, 1)
    tm = min(t, 1024)
    out = jax.ShapeDtypeStruct((t, LANES), F32)
    return pl.pallas_call(
        _rope_kernel,
        out_shape=(out, out),
        grid=(t // tm,),
        in_specs=[pl.BlockSpec((tm, 1), lambda i: (i, 0)), pl.BlockSpec((1, LANES), lambda i: (0, 0))],
        out_specs=(pl.BlockSpec((tm, LANES), lambda i: (i, 0)), pl.BlockSpec((tm, LANES), lambda i: (i, 0))),
        compiler_params=_params(("arbitrary",), _vmem_limit([_nbytes((tm, LANES), F32)] * 3, temp_bytes=8 << 20)),
        name="rope_tables",
    )(pos, freq)


def _rope_slab(z, cos_ref, sin_ref):
    half = MLA_ROPE // 2
    lane = lax.broadcasted_iota(jnp.int32, z.shape, 1)
    partner = jnp.where(lane < half, pltpu.roll(z, LANES - half, axis=1), pltpu.roll(z, half, axis=1))
    return z * cos_ref[...] + partner * sin_ref[...]


def _relu2(y):
    r = jnp.maximum(y, 0.0)
    return r * r


def _mm_norm_kernel(x_ref, mod_ref, g_ref, w_ref, o_ref, h_scr, *, epilogue):
    @pl.when(pl.program_id(1) == 0)
    def _():
        h_scr[...] = _modulated_norm(x_ref, mod_ref, g_ref).astype(BF16)

    y = jnp.dot(h_scr[...], w_ref[...], preferred_element_type=F32)
    o_ref[...] = epilogue(y).astype(o_ref.dtype)


def _mm_norm(x, mod, g, w, seq, *, epilogue, tm=1024, tn=1024):
    t, d = x.shape
    n = w.shape[1]
    tiles_per_batch = seq // tm
    blocks = [_nbytes((tm, d), F32), _nbytes((d, tn), BF16), _nbytes((tm, tn), BF16)]
    return pl.pallas_call(
        functools.partial(_mm_norm_kernel, epilogue=epilogue),
        out_shape=jax.ShapeDtypeStruct((t, n), BF16),
        grid=(t // tm, n // tn),
        in_specs=[
            pl.BlockSpec((tm, d), lambda i, j: (i, 0)),
            pl.BlockSpec((None, 3, d), lambda i, j: (i // tiles_per_batch, 0, 0)),
            pl.BlockSpec((1, d), lambda i, j: (0, 0)),
            pl.BlockSpec((d, tn), lambda i, j: (0, j)),
        ],
        out_specs=pl.BlockSpec((tm, tn), lambda i, j: (i, j)),
        scratch_shapes=[pltpu.VMEM((tm, d), BF16)],
        compiler_params=_params(("arbitrary", "arbitrary"),
                                _vmem_limit(blocks, _nbytes((tm, d), BF16), _nbytes((tm, d), F32) + _nbytes((tm, tn), F32))),
        name="mm_norm",
    )(x, mod, g.reshape(1, d), w)


def _mm_resid_kernel(a_ref, w_ref, x_ref, mod_ref, g_ref, o_ref, *, nk):
    k = pl.program_id(1)
    part = jnp.dot(a_ref[...], w_ref[...], preferred_element_type=F32)

    @pl.when(k == 0)
    def _():
        o_ref[...] = part

    @pl.when(k > 0)
    def _():
        o_ref[...] += part

    @pl.when(k == nk - 1)
    def _():
        y = _rms(o_ref[...]) * g_ref[...]
        o_ref[...] = x_ref[...] + mod_ref[2:3, :] * y


def _mm_resid(a, w, x, mod, g, seq, *, tm=512, tk=1024):
    t, kdim = a.shape
    d = w.shape[1]
    nk = kdim // tk
    tiles_per_batch = seq // tm
    blocks = [_nbytes((tm, tk), BF16), _nbytes((tk, d), BF16), _nbytes((tm, d), F32), _nbytes((tm, d), F32)]
    return pl.pallas_call(
        functools.partial(_mm_resid_kernel, nk=nk),
        out_shape=jax.ShapeDtypeStruct((t, d), F32),
        grid=(t // tm, nk),
        in_specs=[
            pl.BlockSpec((tm, tk), lambda i, k: (i, k)),
            pl.BlockSpec((tk, d), lambda i, k: (k, 0)),
            pl.BlockSpec((tm, d), lambda i, k: (i, 0)),
            pl.BlockSpec((None, 3, d), lambda i, k: (i // tiles_per_batch, 0, 0)),
            pl.BlockSpec((1, d), lambda i, k: (0, 0)),
        ],
        out_specs=pl.BlockSpec((tm, d), lambda i, k: (i, 0)),
        compiler_params=_params(("arbitrary", "arbitrary"), _vmem_limit(blocks, temp_bytes=_nbytes((tm, d), F32))),
        name="mm_resid",
    )(a, w, x, mod, g.reshape(1, d))


def _mla_down_kernel(x_ref, mod_ref, g_ref, w_ref, qn_ref, kvn_ref, cos_ref, sin_ref, cq_ref, ckv_ref, kr_ref):
    h = _modulated_norm(x_ref, mod_ref, g_ref).astype(BF16)
    y = jnp.dot(h, w_ref[...], preferred_element_type=F32)
    cq_ref[...] = (_rms(y[:, :MLA_LORA]) * qn_ref[...]).astype(BF16)
    ckv_ref[...] = (_rms(y[:, MLA_LORA:2 * MLA_LORA]) * kvn_ref[...]).astype(BF16)
    kr_ref[...] = _rope_slab(y[:, 2 * MLA_LORA:], cos_ref, sin_ref).astype(BF16)


def _mla_down(x, mod, g, w_down, q_norm, kv_norm, cos_t, sin_t, seq, *, tm=512):
    t, d = x.shape
    n = w_down.shape[1]
    tiles_per_batch = seq // tm
    blocks = [_nbytes((tm, d), F32), _nbytes((d, n), BF16), 3 * _nbytes((tm, MLA_LORA), BF16), 2 * _nbytes((tm, LANES), F32)]
    row = lambda i: (i, 0)
    fixed = lambda i: (0, 0)
    return pl.pallas_call(
        _mla_down_kernel,
        out_shape=(jax.ShapeDtypeStruct((t, MLA_LORA), BF16), jax.ShapeDtypeStruct((t, MLA_LORA), BF16),
                   jax.ShapeDtypeStruct((t, LANES), BF16)),
        grid=(t // tm,),
        in_specs=[
            pl.BlockSpec((tm, d), row),
            pl.BlockSpec((None, 3, d), lambda i: (i // tiles_per_batch, 0, 0)),
            pl.BlockSpec((1, d), fixed),
            pl.BlockSpec((d, n), fixed),
            pl.BlockSpec((1, MLA_LORA), fixed),
            pl.BlockSpec((1, MLA_LORA), fixed),
            pl.BlockSpec((tm, LANES), row),
            pl.BlockSpec((tm, LANES), row),
        ],
        out_specs=(pl.BlockSpec((tm, MLA_LORA), row), pl.BlockSpec((tm, MLA_LORA), row), pl.BlockSpec((tm, LANES), row)),
        compiler_params=_params(("arbitrary",), _vmem_limit(blocks, temp_bytes=2 * _nbytes((tm, d), F32))),
        name="mla_down",
    )(x, mod, g.reshape(1, d), w_down, q_norm.reshape(1, -1), kv_norm.reshape(1, -1), cos_t, sin_t)


def _q_up_kernel(a_ref, w_ref, cos_ref, sin_ref, q_ref, *, heads, scale):
    y = jnp.dot(a_ref[...], w_ref[...], preferred_element_type=F32) * scale
    for h in range(heads):
        base = h * HEAD_PAD
        q_ref[h, :, :LANES] = y[:, base:base + LANES].astype(BF16)
        q_ref[h, :, LANES:] = _rope_slab(y[:, base + LANES:base + HEAD_PAD], cos_ref, sin_ref).astype(BF16)


def _kv_up_kernel(a_ref, w_ref, kr_ref, k_ref, v_ref, *, heads):
    y = jnp.dot(a_ref[...], w_ref[...], preferred_element_type=F32)
    kr = kr_ref[...]
    for h in range(heads):
        base = h * (MLA_NOPE + MLA_V)
        k_ref[h, :, :LANES] = y[:, base:base + MLA_NOPE].astype(BF16)
        k_ref[h, :, LANES:] = kr
        v_ref[h] = y[:, base + MLA_NOPE:base + MLA_NOPE + MLA_V].astype(BF16)


def _q_up(cq, w_uq, cos_t, sin_t, bsz, seq, *, tm=1024, heads=4):
    t, r = cq.shape
    tiles_per_batch = seq // tm
    scale = 1.0 / math.sqrt(MLA_NOPE + MLA_ROPE)
    tn = heads * HEAD_PAD
    blocks = [_nbytes((tm, r), BF16), _nbytes((r, tn), BF16), 2 * _nbytes((tm, LANES), F32), _nbytes((heads, tm, HEAD_PAD), BF16)]
    return pl.pallas_call(
        functools.partial(_q_up_kernel, heads=heads, scale=scale),
        out_shape=jax.ShapeDtypeStruct((bsz, MLA_HEADS, seq, HEAD_PAD), BF16),
        grid=(t // tm, MLA_HEADS // heads),
        in_specs=[
            pl.BlockSpec((tm, r), lambda i, j: (i, 0)),
            pl.BlockSpec((r, tn), lambda i, j: (0, j)),
            pl.BlockSpec((tm, LANES), lambda i, j: (i, 0)),
            pl.BlockSpec((tm, LANES), lambda i, j: (i, 0)),
        ],
        out_specs=pl.BlockSpec((None, heads, tm, HEAD_PAD),
                               lambda i, j: (i // tiles_per_batch, j, i % tiles_per_batch, 0)),
        compiler_params=_params(("arbitrary", "arbitrary"), _vmem_limit(blocks, temp_bytes=2 * _nbytes((tm, tn), F32))),
        name="mla_q_up",
    )(cq, w_uq, cos_t, sin_t)


def _kv_up(ckv, w_ukv, kr, bsz, seq, *, tm=1024, heads=4):
    t, r = ckv.shape
    tiles_per_batch = seq // tm
    tn = heads * (MLA_NOPE + MLA_V)
    blocks = [_nbytes((tm, r), BF16), _nbytes((r, tn), BF16), _nbytes((tm, LANES), BF16),
              _nbytes((heads, tm, HEAD_PAD), BF16), _nbytes((heads, tm, MLA_V), BF16)]
    out_map = lambda i, j: (i // tiles_per_batch, j, i % tiles_per_batch, 0)
    return pl.pallas_call(
        functools.partial(_kv_up_kernel, heads=heads),
        out_shape=(jax.ShapeDtypeStruct((bsz, MLA_HEADS, seq, HEAD_PAD), BF16),
                   jax.ShapeDtypeStruct((bsz, MLA_HEADS, seq, MLA_V), BF16)),
        grid=(t // tm, MLA_HEADS // heads),
        in_specs=[
            pl.BlockSpec((tm, r), lambda i, j: (i, 0)),
            pl.BlockSpec((r, tn), lambda i, j: (0, j)),
            pl.BlockSpec((tm, LANES), lambda i, j: (i, 0)),
        ],
        out_specs=(pl.BlockSpec((None, heads, tm, HEAD_PAD), out_map), pl.BlockSpec((None, heads, tm, MLA_V), out_map)),
        compiler_params=_params(("arbitrary", "arbitrary"), _vmem_limit(blocks, temp_bytes=2 * _nbytes((tm, tn), F32))),
        name="mla_kv_up",
    )(ckv, w_ukv, kr)


def _attn_kernel(q_ref, k_ref, v_ref, o_ref, m_scr, l_scr, acc_scr, *, tq, tk):
    qi = pl.program_id(2)
    q = q_ref[...]
    m_scr[...] = jnp.full(m_scr.shape, -jnp.inf, F32)
    l_scr[...] = jnp.zeros(l_scr.shape, F32)
    acc_scr[...] = jnp.zeros(acc_scr.shape, F32)

    def step(off, mask):
        kb = k_ref[pl.ds(off, tk), :]
        vb = v_ref[pl.ds(off, tk), :]
        s = lax.dot_general(q, kb, (((1,), (1,)), ((), ())), preferred_element_type=F32)
        if mask is not None:
            s = jnp.where(mask, s, -jnp.inf)
        m_prev = m_scr[...]
        m_new = jnp.maximum(m_prev, jnp.max(s, axis=-1, keepdims=True))
        alpha = jnp.exp(m_prev - m_new)
        p = jnp.exp(s - m_new)
        l_scr[...] = alpha * l_scr[...] + jnp.sum(p, axis=-1, keepdims=True)
        acc_scr[...] = alpha * acc_scr[...] + jnp.dot(p.astype(BF16), vb, preferred_element_type=F32)
        m_scr[...] = m_new

    def body(j, carry):
        step(pl.multiple_of(j * tk, tk), None)
        return carry

    lax.fori_loop(0, qi * (tq // tk), body, 0)
    q_chunk = lax.broadcasted_iota(jnp.int32, (tq, tk), 0) // CHUNK
    k_chunk = lax.broadcasted_iota(jnp.int32, (tq, tk), 1) // CHUNK
    for d in range(tq // tk):
        step(pl.multiple_of(qi * tq + d * tk, tk), k_chunk + (d * tk) // CHUNK <= q_chunk)
    o_ref[...] = (acc_scr[...] / l_scr[...]).astype(o_ref.dtype)


def _attention(q, k, v, *, tq=512, tk=512):
    bsz, heads, seq, _ = q.shape
    nq = seq // tq
    blocks = [_nbytes((tq, HEAD_PAD), BF16), _nbytes((seq, HEAD_PAD), BF16), _nbytes((seq, MLA_V), BF16), _nbytes((tq, MLA_V), BF16)]
    scratch = 2 * _nbytes((tq, LANES), F32) + _nbytes((tq, MLA_V), F32)
    return pl.pallas_call(
        functools.partial(_attn_kernel, tq=tq, tk=tk),
        out_shape=jax.ShapeDtypeStruct((bsz * seq, heads * MLA_V), BF16),
        grid=(bsz, heads, nq),
        in_specs=[
            pl.BlockSpec((None, None, tq, HEAD_PAD), lambda b, h, i: (b, h, i, 0)),
            pl.BlockSpec((None, None, seq, HEAD_PAD), lambda b, h, i: (b, h, 0, 0)),
            pl.BlockSpec((None, None, seq, MLA_V), lambda b, h, i: (b, h, 0, 0)),
        ],
        out_specs=pl.BlockSpec((tq, MLA_V), lambda b, h, i: (b * nq + i, h)),
        scratch_shapes=[pltpu.VMEM((tq, 1), F32), pltpu.VMEM((tq, 1), F32), pltpu.VMEM((tq, MLA_V), F32)],
        compiler_params=_params(("arbitrary", "arbitrary", "arbitrary"),
                                _vmem_limit(blocks, scratch, 6 * _nbytes((tq, tk), F32))),
        name="mla_attention",
    )(q, k, v)


def _causal_conv(t, tail_ref, buf_ref, w_ref, first_of_batch):
    kw = w_ref.shape[0]
    tm = t.shape[0]
    hist = jnp.where(first_of_batch, 0.0, tail_ref[...])
    buf_ref[pl.ds(0, SUBLANES), :] = hist
    buf_ref[pl.ds(SUBLANES, tm), :] = t
    tail_ref[...] = t[tm - SUBLANES:, :]
    acc = t * w_ref[kw - 1:kw, :]
    for k in range(kw - 1):
        acc = acc + buf_ref[pl.ds(SUBLANES - (kw - 1) + k, tm), :] * w_ref[k:k + 1, :]
    return acc


def _conv_in_kernel(x_ref, mod_ref, g_ref, wb_ref, wc_ref, wu_ref, cw_ref, o_ref, h_scr, tail_scr, buf_scr, *, tiles_per_batch):
    i, j = pl.program_id(0), pl.program_id(1)

    @pl.when(j == 0)
    def _():
        h_scr[...] = _modulated_norm(x_ref, mod_ref, g_ref).astype(BF16)

    h = h_scr[...]
    gc = jnp.dot(h, wc_ref[...], preferred_element_type=F32)
    u = jnp.dot(h, wu_ref[...], preferred_element_type=F32)
    conv = _causal_conv(gc * u, tail_scr.at[j], buf_scr, cw_ref, i % tiles_per_batch == 0)
    gb = jnp.dot(h, wb_ref[...], preferred_element_type=F32)
    o_ref[...] = (gb * conv).astype(o_ref.dtype)


def _conv_in(x, mod, g, w_in, conv_w, seq, *, tm=512, tn=512):
    t, d = x.shape
    n = w_in.shape[1] // 3
    nj = n // tn
    tiles_per_batch = seq // tm
    kw = conv_w.shape[0]
    blocks = [_nbytes((tm, d), F32), 3 * _nbytes((d, tn), BF16), _nbytes((tm, tn), BF16)]
    scratch = _nbytes((tm, d), BF16) + _nbytes((nj, SUBLANES, tn), F32) + _nbytes((tm + SUBLANES, tn), F32)
    wspec = lambda off: pl.BlockSpec((d, tn), lambda i, j: (0, j + off * nj))
    return pl.pallas_call(
        functools.partial(_conv_in_kernel, tiles_per_batch=tiles_per_batch),
        out_shape=jax.ShapeDtypeStruct((t, n), BF16),
        grid=(t // tm, nj),
        in_specs=[
            pl.BlockSpec((tm, d), lambda i, j: (i, 0)),
            pl.BlockSpec((None, 3, d), lambda i, j: (i // tiles_per_batch, 0, 0)),
            pl.BlockSpec((1, d), lambda i, j: (0, 0)),
            wspec(0), wspec(1), wspec(2),
            pl.BlockSpec((kw, tn), lambda i, j: (0, j)),
        ],
        out_specs=pl.BlockSpec((tm, tn), lambda i, j: (i, j)),
        scratch_shapes=[pltpu.VMEM((tm, d), BF16), pltpu.VMEM((nj, SUBLANES, tn), F32), pltpu.VMEM((tm + SUBLANES, tn), F32)],
        compiler_params=_params(("arbitrary", "arbitrary"),
                                _vmem_limit(blocks, scratch, _nbytes((tm, d), F32) + 6 * _nbytes((tm, tn), F32))),
        name="conv_in",
    )(x, mod, g.reshape(1, d), w_in, w_in, w_in, conv_w)


def _softplus(x):
    return jnp.maximum(x, 0.0) + jnp.log1p(jnp.exp(-jnp.abs(x)))


def _ssd_in_kernel(x_ref, mod_ref, g_ref, w_ref, wdt_ref, dtb_ref, cw_ref, cb_ref, o_ref, dt_ref,
                   h_scr, tail_scr, buf_scr, *, tiles_per_batch, nz):
    i, j = pl.program_id(0), pl.program_id(1)

    @pl.when(j == 0)
    def _():
        h = _modulated_norm(x_ref, mod_ref, g_ref).astype(BF16)
        h_scr[...] = h
        dt_ref[...] = _softplus(jnp.dot(h, wdt_ref[...], preferred_element_type=F32) + dtb_ref[...])

    y = jnp.dot(h_scr[...], w_ref[...], preferred_element_type=F32)

    @pl.when(j < nz)
    def _():
        o_ref[...] = y.astype(o_ref.dtype)

    @pl.when(j >= nz)
    def _():
        jc = j - nz
        conv = _causal_conv(y, tail_scr.at[jc], buf_scr, cw_ref, i % tiles_per_batch == 0) + cb_ref[...]
        o_ref[...] = _silu(conv).astype(o_ref.dtype)


def _ssd_in(x, mod, g, w_main, w_dt, dt_bias, conv_w, conv_b, d_inner, seq, *, tm=512, tn=512):
    t, d = x.shape
    n = w_main.shape[1]
    nh = w_dt.shape[1]
    nj = n // tn
    nz = d_inner // tn
    nconv = nj - nz
    tiles_per_batch = seq // tm
    kw = conv_w.shape[0]
    blocks = [_nbytes((tm, d), F32), _nbytes((d, tn), BF16), _nbytes((d, nh), BF16), _nbytes((tm, tn), BF16), _nbytes((tm, LANES), F32)]
    scratch = _nbytes((tm, d), BF16) + _nbytes((nconv, SUBLANES, tn), F32) + _nbytes((tm + SUBLANES, tn), F32)
    cmap = lambda i, j: (0, jnp.maximum(j - nz, 0))
    return pl.pallas_call(
        functools.partial(_ssd_in_kernel, tiles_per_batch=tiles_per_batch, nz=nz),
        out_shape=(jax.ShapeDtypeStruct((t, n), BF16), jax.ShapeDtypeStruct((t, nh), F32)),
        grid=(t // tm, nj),
        in_specs=[
            pl.BlockSpec((tm, d), lambda i, j: (i, 0)),
            pl.BlockSpec((None, 3, d), lambda i, j: (i // tiles_per_batch, 0, 0)),
            pl.BlockSpec((1, d), lambda i, j: (0, 0)),
            pl.BlockSpec((d, tn), lambda i, j: (0, j)),
            pl.BlockSpec((d, nh), lambda i, j: (0, 0)),
            pl.BlockSpec((1, nh), lambda i, j: (0, 0)),
            pl.BlockSpec((kw, tn), cmap),
            pl.BlockSpec((1, tn), cmap),
        ],
        out_specs=(pl.BlockSpec((tm, tn), lambda i, j: (i, j)), pl.BlockSpec((tm, nh), lambda i, j: (i, 0))),
        scratch_shapes=[pltpu.VMEM((tm, d), BF16), pltpu.VMEM((nconv, SUBLANES, tn), F32), pltpu.VMEM((tm + SUBLANES, tn), F32)],
        compiler_params=_params(("arbitrary", "arbitrary"),
                                _vmem_limit(blocks, scratch, _nbytes((tm, d), F32) + 6 * _nbytes((tm, tn), F32))),
        name="ssd_in",
    )(x, mod, g.reshape(1, d), w_main, w_dt, dt_bias.reshape(1, nh), conv_w, conv_b.reshape(1, -1))


def _split3(x):
    hi = x.astype(BF16)
    r1 = x - hi.astype(F32)
    mid = r1.astype(BF16)
    lo = (r1 - mid.astype(F32)).astype(BF16)
    return hi, mid, lo


def _dot01(sel, x, *, sel_left):
    out = None
    for part in _split3(x):
        d = (jnp.dot(sel, part, preferred_element_type=F32) if sel_left
             else jnp.dot(part, sel, preferred_element_type=F32))
        out = d if out is None else out + d
    return out


def _ssd_kernel(z_ref, x_ref, b_ref, c_ref, dt_ref, alog_ref, dskip_ref, nw_ref, o_ref, state_scr, *, tt):
    hg = dt_ref.shape[-1]
    gw = x_ref.shape[-1]
    hd = gw // hg
    nch = tt // CHUNK

    @pl.when(pl.program_id(2) == 0)
    def _():
        state_scr[...] = jnp.zeros(state_scr.shape, F32)

    def iota(shape, axis):
        return lax.broadcasted_iota(jnp.int32, shape, axis)

    expand = (iota((hg, gw), 1) // hd == iota((hg, gw), 0)).astype(BF16)
    r2, c2 = iota((tt, tt), 0), iota((tt, tt), 1)
    tri = ((c2 <= r2) & (c2 // CHUNK == r2 // CHUNK)).astype(BF16)
    lrow, lcol = iota((CHUNK, gw), 0), iota((CHUNK, gw), 1) % hd
    half = gw // 2
    bd_keep = iota((half * CHUNK // hd, half), 0) // CHUNK == iota((half * CHUNK // hd, half), 1) // hd

    a_row = -jnp.exp(alog_ref[...])
    dt_all = _dot01(expand, dt_ref[...], sel_left=False)
    dta_all = dt_all * a_row
    acum_all = _dot01(tri, dta_all, sel_left=True)

    for ci in range(nch):
        rows = slice(ci * CHUNK, (ci + 1) * CHUNK)
        x = x_ref[rows, :].astype(F32)
        bm = b_ref[rows, :]
        cm = c_ref[rows, :]
        dt, dta, acum = dt_all[rows], dta_all[rows], acum_all[rows]
        r = jnp.sum(jnp.where(lrow <= lcol, dta, 0.0), axis=0, keepdims=True)
        decay = jnp.exp(jnp.where(lrow >= lcol, acum - r, -jnp.inf))
        cb = lax.dot_general(cm, bm, (((1,), (1,)), ((), ())), preferred_element_type=F32)
        m = (jnp.concatenate([cb] * hg, axis=1) * decay).astype(BF16)
        xdt = x * dt
        xdt16 = xdt.astype(BF16)
        ydiag = []
        for hf in range(2):
            cols = slice(hf * half, (hf + 1) * half)
            blockdiag = jnp.where(bd_keep, jnp.concatenate([xdt16[:, cols]] * (half // hd), axis=0), jnp.zeros((), BF16))
            ydiag.append(jnp.dot(m[:, cols], blockdiag, preferred_element_type=F32))
        last = acum[CHUNK - 1:CHUNK, :]
        xw = (xdt * jnp.exp(last - acum)).astype(BF16)
        st = lax.dot_general(bm, xw, (((0,), (0,)), ((), ())), preferred_element_type=F32)
        prev = state_scr[...]
        yoff = jnp.dot(cm, prev.astype(BF16), preferred_element_type=F32) * jnp.exp(acum)
        state_scr[...] = prev * jnp.exp(last) + st
        y = jnp.concatenate(ydiag, axis=1) + yoff + x * dskip_ref[...]
        gated = y * _silu(z_ref[rows, :].astype(F32))
        o_ref[rows, :] = (_rms(gated) * nw_ref[...]).astype(o_ref.dtype)


def _ssd_scan(zx, dt_g, a_log, d_skip, norm_w, bsz, seq, d_inner, *, tt=256):
    t = zx.shape[0]
    groups, _, hg = dt_g.shape
    gw = d_inner // groups
    n = SSM_STATE
    nt = seq // tt
    zoff, xoff = 0, d_inner // gw
    boff = 2 * d_inner // n
    coff = boff + groups
    row = lambda b, g, i: b * nt + i
    blocks = [2 * _nbytes((tt, gw), BF16), 2 * _nbytes((tt, n), BF16), _nbytes((tt, LANES), F32), _nbytes((tt, gw), BF16)]
    return pl.pallas_call(
        functools.partial(_ssd_kernel, tt=tt),
        out_shape=jax.ShapeDtypeStruct((t, d_inner), BF16),
        grid=(bsz, groups, nt),
        in_specs=[
            pl.BlockSpec((tt, gw), lambda b, g, i: (row(b, g, i), zoff + g)),
            pl.BlockSpec((tt, gw), lambda b, g, i: (row(b, g, i), xoff + g)),
            pl.BlockSpec((tt, n), lambda b, g, i: (row(b, g, i), boff + g)),
            pl.BlockSpec((tt, n), lambda b, g, i: (row(b, g, i), coff + g)),
            pl.BlockSpec((None, tt, hg), lambda b, g, i: (g, row(b, g, i), 0)),
            pl.BlockSpec((None, 1, gw), lambda b, g, i: (g, 0, 0)),
            pl.BlockSpec((None, 1, gw), lambda b, g, i: (g, 0, 0)),
            pl.BlockSpec((1, gw), lambda b, g, i: (0, g)),
        ],
        out_specs=pl.BlockSpec((tt, gw), lambda b, g, i: (row(b, g, i), g)),
        scratch_shapes=[pltpu.VMEM((n, gw), F32)],
        compiler_params=_params(("arbitrary", "arbitrary", "arbitrary"),
                                _vmem_limit(blocks, _nbytes((n, gw), F32), 24 * _nbytes((tt, gw), F32))),
        name="ssd_scan",
    )(zx, zx, zx, zx, dt_g, a_log, d_skip, norm_w.reshape(1, d_inner))


def _pad_heads(w, heads, width, padded):
    r = w.shape[0]
    w = w.reshape(r, heads, width)
    return jnp.pad(w, ((0, 0), (0, 0), (0, padded - width))).reshape(r, heads * padded)


def kernel(x, c, positions, ada_w, ada_b, norm_pre, norm_post, mla_w_down, mla_q_norm, mla_w_uq, mla_kv_norm, mla_w_ukv, mla_w_o, conv_w_in, conv_w, conv_w_out, ssm_w_in, ssm_conv_w, ssm_conv_b, ssm_dt_bias, ssm_a_log, ssm_d, ssm_norm, ssm_w_out, mlp_up, mlp_down):
    bsz, seq, d = x.shape
    depth = ada_w.shape[0]
    t = bsz * seq
    xf = x.reshape(t, d)

    mods = _ada_all(c, ada_w, ada_b).reshape(depth, 2, bsz, 3, d)
    cos_t, sin_t = _rope_tables(positions)

    for i in range(depth):
        kind, j = i % N_MIXERS, i // N_MIXERS
        mod, g_pre, g_post = mods[i, 0], norm_pre[i, 0], norm_post[i, 0]
        if kind == 0:
            w_down = jnp.pad(mla_w_down[j], ((0, 0), (0, LANES - MLA_ROPE))).astype(BF16)
            w_uq = _pad_heads(mla_w_uq[j], MLA_HEADS, MLA_NOPE + MLA_ROPE, HEAD_PAD).astype(BF16)
            cq, ckv, kr = _mla_down(xf, mod, g_pre, w_down, mla_q_norm[j], mla_kv_norm[j], cos_t, sin_t, seq)
            q = _q_up(cq, w_uq, cos_t, sin_t, bsz, seq)
            k, v = _kv_up(ckv, mla_w_ukv[j].astype(BF16), kr, bsz, seq)
            a = _attention(q, k, v)
            w_o = mla_w_o[j]
        elif kind == 1:
            a = _conv_in(xf, mod, g_pre, conv_w_in[j].astype(BF16), conv_w[j], seq)
            w_o = conv_w_out[j]
        else:
            d_inner = ssm_w_out.shape[1]
            nh = ssm_dt_bias.shape[1]
            n_main = ssm_w_in.shape[2] - nh
            zx, dt = _ssd_in(xf, mod, g_pre, ssm_w_in[j, :, :n_main].astype(BF16), ssm_w_in[j, :, n_main:].astype(BF16),
                             ssm_dt_bias[j], ssm_conv_w[j], ssm_conv_b[j], d_inner, seq)
            hg = nh // SSM_GROUPS
            gw = d_inner // SSM_GROUPS
            dt_g = dt.reshape(t, SSM_GROUPS, hg).transpose(1, 0, 2)
            per_lane = lambda p: jnp.repeat(p.reshape(SSM_GROUPS, hg), SSM_HEAD_DIM, axis=1).reshape(SSM_GROUPS, 1, gw)
            a = _ssd_scan(zx, dt_g, per_lane(ssm_a_log[j]), per_lane(ssm_d[j]), ssm_norm[j], bsz, seq, d_inner)
            w_o = ssm_w_out[j]
        xf = _mm_resid(a, w_o.astype(BF16), xf, mod, g_post, seq)

        mod, g_pre, g_post = mods[i, 1], norm_pre[i, 1], norm_post[i, 1]
        u = _mm_norm(xf, mod, g_pre, mlp_up[i].astype(BF16), seq, epilogue=_relu2)
        xf = _mm_resid(u, mlp_down[i].astype(BF16), xf, mod, g_post, seq)
    return xf.reshape(bsz, seq, d)
```

```python
import functools
import math

import jax
import jax.numpy as jnp
from jax import lax
from jax.experimental import pallas as pl
from jax.experimental.pallas import tpu as pltpu

F32 = jnp.float32
BF16 = jnp.bfloat16

EPS = 1e-6
CHUNK = 64
N_MIXERS = 3
MLA_HEADS = 16
MLA_LORA = 512
MLA_NOPE = 128
MLA_ROPE = 64
MLA_V = 128
ROPE_THETA = 10000.0
SSM_HEAD_DIM = 64
SSM_GROUPS = 8
SSM_STATE = 128

LANES = 128
SUBLANES = 8
V7X_VMEM_BYTES = 64 * 1024 * 1024
HEAD_PAD = 2 * LANES


def _vmem_limit(block_bytes, scratch_bytes=0, temp_bytes=0):
    need = 2 * sum(block_bytes) + scratch_bytes + temp_bytes + (4 << 20)
    return int(min(need, V7X_VMEM_BYTES - (6 << 20)))


def _nbytes(shape, dtype):
    return math.prod(shape) * jnp.dtype(dtype).itemsize


def _params(sem, limit):
    return pltpu.CompilerParams(dimension_semantics=sem, vmem_limit_bytes=limit)


def _rms(x):
    return x * lax.rsqrt(jnp.mean(x * x, axis=-1, keepdims=True) + EPS)


def _modulated_norm(x_ref, mod_ref, g_ref):
    h = _rms(x_ref[...]) * g_ref[...]
    return h * (1.0 + mod_ref[1:2, :]) + mod_ref[0:1, :]


def _silu(x):
    return x * (1.0 / (1.0 + jnp.exp(-x)))


def _ada_kernel(c_ref, w_ref, b_ref, o_ref):
    c_act = _silu(c_ref[...]).astype(BF16)
    y = jnp.dot(c_act, w_ref[...].astype(BF16), preferred_element_type=F32)
    o_ref[...] = y + b_ref[...]


def _ada_all(c, ada_w, ada_b):
    nsub = ada_w.shape[0] * ada_w.shape[1]
    bsz, d = c.shape
    n = ada_w.shape[-1]
    w = ada_w.reshape(nsub, d, n)
    b = ada_b.reshape(nsub, 1, n)
    tn = 1024
    blocks = [_nbytes((d, tn), F32), _nbytes((bsz, tn), F32)]
    return pl.pallas_call(
        _ada_kernel,
        out_shape=jax.ShapeDtypeStruct((nsub, bsz, n), F32),
        grid=(nsub, n // tn),
        in_specs=[
            pl.BlockSpec((bsz, d), lambda s, j: (0, 0)),
            pl.BlockSpec((None, d, tn), lambda s, j: (s, 0, j)),
            pl.BlockSpec((None, 1, tn), lambda s, j: (s, 0, j)),
        ],
        out_specs=pl.BlockSpec((None, bsz, tn), lambda s, j: (s, 0, j)),
        compiler_params=_params(("arbitrary", "arbitrary"), _vmem_limit(blocks, temp_bytes=_nbytes((d, tn), BF16))),
        name="ada_mod",
    )(c, w, b)


def _rope_kernel(pos_ref, freq_ref, cos_ref, sin_ref):
    ang = pos_ref[...] * freq_ref[...]
    lane = lax.broadcasted_iota(jnp.int32, ang.shape, 1)
    half = MLA_ROPE // 2
    cos_ref[...] = jnp.where(lane < MLA_ROPE, jnp.cos(ang), 0.0)
    s = jnp.sin(ang)
    sin_ref[...] = jnp.where(lane < half, -s, jnp.where(lane < MLA_ROPE, s, 0.0))


def _rope_tables(positions):
    t = positions.size
    half = MLA_ROPE // 2
    inv_freq = ROPE_THETA ** (-jnp.arange(0, MLA_ROPE, 2, dtype=F32) / MLA_ROPE)
    freq = jnp.concatenate([inv_freq, inv_freq, jnp.zeros((LANES - 2 * half,), F32)]).reshape(1, LANES)
    pos = positions.astype(F32).reshape(t, 1)
    tm = min(t, 1024)
    out = jax.ShapeDtypeStruct((t, LANES), F32)
    return pl.pallas_call(
        _rope_kernel,
        out_shape=(out, out),
        grid=(t // tm,),
        in_specs=[pl.BlockSpec((tm, 1), lambda i: (i, 0)), pl.BlockSpec((1, LANES), lambda i: (0, 0))],
        out_specs=(pl.BlockSpec((tm, LANES), lambda i: (i, 0)), pl.BlockSpec((tm, LANES), lambda i: (i, 0))),
        compiler_params=_params(("arbitrary",), _vmem_limit([_nbytes((tm, LANES), F32)] * 3, temp_bytes=8 << 20)),
        name="rope_tables",
    )(pos, freq)


def _rope_slab(z, cos_ref, sin_ref):
    half = MLA_ROPE // 2
    lane = lax.broadcasted_iota(jnp.int32, z.shape, 1)
    partner = jnp.where(lane < half, pltpu.roll(z, LANES - half, axis=1), pltpu.roll(z, half, axis=1))
    return z * cos_ref[...] + partner * sin_ref[...]


def _relu2(y):
    r = jnp.maximum(y, 0.0)
    return r * r


def _mm_norm_kernel(x_ref, mod_ref, g_ref, w_ref, o_ref, h_scr, *, epilogue):
    @pl.when(pl.program_id(1) == 0)
    def _():
        h_scr[...] = _modulated_norm(x_ref, mod_ref, g_ref).astype(BF16)

    y = jnp.dot(h_scr[...], w_ref[...], preferred_element_type=F32)
    o_ref[...] = epilogue(y).astype(o_ref.dtype)


def _mm_norm(x, mod, g, w, seq, *, epilogue, tm=1024, tn=1024):
    t, d = x.shape
    n = w.shape[1]
    tiles_per_batch = seq // tm
    blocks = [_nbytes((tm, d), F32), _nbytes((d, tn), BF16), _nbytes((tm, tn), BF16)]
    return pl.pallas_call(
        functools.partial(_mm_norm_kernel, epilogue=epilogue),
        out_shape=jax.ShapeDtypeStruct((t, n), BF16),
        grid=(t // tm, n // tn),
        in_specs=[
            pl.BlockSpec((tm, d), lambda i, j: (i, 0)),
            pl.BlockSpec((None, 3, d), lambda i, j: (i // tiles_per_batch, 0, 0)),
            pl.BlockSpec((1, d), lambda i, j: (0, 0)),
            pl.BlockSpec((d, tn), lambda i, j: (0, j)),
        ],
        out_specs=pl.BlockSpec((tm, tn), lambda i, j: (i, j)),
        scratch_shapes=[pltpu.VMEM((tm, d), BF16)],
        compiler_params=_params(("arbitrary", "arbitrary"),
                                _vmem_limit(blocks, _nbytes((tm, d), BF16), _nbytes((tm, d), F32) + _nbytes((tm, tn), F32))),
        name="mm_norm",
    )(x, mod, g.reshape(1, d), w)


def _mm_resid_kernel(a_ref, w_ref, x_ref, mod_ref, g_ref, o_ref, *, nk):
    k = pl.program_id(1)
    part = jnp.dot(a_ref[...], w_ref[...], preferred_element_type=F32)

    @pl.when(k == 0)
    def _():
        o_ref[...] = part

    @pl.when(k > 0)
    def _():
        o_ref[...] += part

    @pl.when(k == nk - 1)
    def _():
        y = _rms(o_ref[...]) * g_ref[...]
        o_ref[...] = x_ref[...] + mod_ref[2:3, :] * y


def _mm_resid(a, w, x, mod, g, seq, *, tm=512, tk=1024):
    t, kdim = a.shape
    d = w.shape[1]
    nk = kdim // tk
    tiles_per_batch = seq // tm
    blocks = [_nbytes((tm, tk), BF16), _nbytes((tk, d), BF16), _nbytes((tm, d), F32), _nbytes((tm, d), F32)]
    return pl.pallas_call(
        functools.partial(_mm_resid_kernel, nk=nk),
        out_shape=jax.ShapeDtypeStruct((t, d), F32),
        grid=(t // tm, nk),
        in_specs=[
            pl.BlockSpec((tm, tk), lambda i, k: (i, k)),
            pl.BlockSpec((tk, d), lambda i, k: (k, 0)),
            pl.BlockSpec((tm, d), lambda i, k: (i, 0)),
            pl.BlockSpec((None, 3, d), lambda i, k: (i // tiles_per_batch, 0, 0)),
            pl.BlockSpec((1, d), lambda i, k: (0, 0)),
        ],
        out_specs=pl.BlockSpec((tm, d), lambda i, k: (i, 0)),
        compiler_params=_params(("arbitrary", "arbitrary"), _vmem_limit(blocks, temp_bytes=_nbytes((tm, d), F32))),
        name="mm_resid",
    )(a, w, x, mod, g.reshape(1, d))


def _mla_down_kernel(x_ref, mod_ref, g_ref, w_ref, qn_ref, kvn_ref, cos_ref, sin_ref, cq_ref, ckv_ref, kr_ref):
    h = _modulated_norm(x_ref, mod_ref, g_ref).astype(BF16)
    y = jnp.dot(h, w_ref[...], preferred_element_type=F32)
    cq_ref[...] = (_rms(y[:, :MLA_LORA]) * qn_ref[...]).astype(BF16)
    ckv_ref[...] = (_rms(y[:, MLA_LORA:2 * MLA_LORA]) * kvn_ref[...]).astype(BF16)
    kr_ref[...] = _rope_slab(y[:, 2 * MLA_LORA:], cos_ref, sin_ref).astype(BF16)


def _mla_down(x, mod, g, w_down, q_norm, kv_norm, cos_t, sin_t, seq, *, tm=512):
    t, d = x.shape
    n = w_down.shape[1]
    tiles_per_batch = seq // tm
    blocks = [_nbytes((tm, d), F32), _nbytes((d, n), BF16), 3 * _nbytes((tm, MLA_LORA), BF16), 2 * _nbytes((tm, LANES), F32)]
    row = lambda i: (i, 0)
    fixed = lambda i: (0, 0)
    return pl.pallas_call(
        _mla_down_kernel,
        out_shape=(jax.ShapeDtypeStruct((t, MLA_LORA), BF16), jax.ShapeDtypeStruct((t, MLA_LORA), BF16),
                   jax.ShapeDtypeStruct((t, LANES), BF16)),
        grid=(t // tm,),
        in_specs=[
            pl.BlockSpec((tm, d), row),
            pl.BlockSpec((None, 3, d), lambda i: (i // tiles_per_batch, 0, 0)),
            pl.BlockSpec((1, d), fixed),
            pl.BlockSpec((d, n), fixed),
            pl.BlockSpec((1, MLA_LORA), fixed),
            pl.BlockSpec((1, MLA_LORA), fixed),
            pl.BlockSpec((tm, LANES), row),
            pl.BlockSpec((tm, LANES), row),
        ],
        out_specs=(pl.BlockSpec((tm, MLA_LORA), row), pl.BlockSpec((tm, MLA_LORA), row), pl.BlockSpec((tm, LANES), row)),
        compiler_params=_params(("arbitrary",), _vmem_limit(blocks, temp_bytes=2 * _nbytes((tm, d), F32))),
        name="mla_down",
    )(x, mod, g.reshape(1, d), w_down, q_norm.reshape(1, -1), kv_norm.reshape(1, -1), cos_t, sin_t)


def _q_up_kernel(a_ref, w_ref, cos_ref, sin_ref, q_ref, *, heads, scale):
    y = jnp.dot(a_ref[...], w_ref[...], preferred_element_type=F32) * scale
    for h in range(heads):
        base = h * HEAD_PAD
        q_ref[h, :, :LANES] = y[:, base:base + LANES].astype(BF16)
        q_ref[h, :, LANES:] = _rope_slab(y[:, base + LANES:base + HEAD_PAD], cos_ref, sin_ref).astype(BF16)


def _kv_up_kernel(a_ref, w_ref, kr_ref, k_ref, vt_ref, *, heads):
    y = jnp.dot(a_ref[...], w_ref[...], preferred_element_type=F32)
    kr = kr_ref[...]
    for h in range(heads):
        base = h * (MLA_NOPE + MLA_V)
        k_ref[h, :, :LANES] = y[:, base:base + MLA_NOPE].astype(BF16)
        k_ref[h, :, LANES:] = kr
        vt_ref[h] = y[:, base + MLA_NOPE:base + MLA_NOPE + MLA_V].T.astype(BF16)


def _q_up(cq, w_uq, cos_t, sin_t, bsz, seq, *, tm=1024, heads=4):
    t, r = cq.shape
    tiles_per_batch = seq // tm
    scale = math.log2(math.e) / math.sqrt(MLA_NOPE + MLA_ROPE)
    tn = heads * HEAD_PAD
    blocks = [_nbytes((tm, r), BF16), _nbytes((r, tn), BF16), 2 * _nbytes((tm, LANES), F32), _nbytes((heads, tm, HEAD_PAD), BF16)]
    return pl.pallas_call(
        functools.partial(_q_up_kernel, heads=heads, scale=scale),
        out_shape=jax.ShapeDtypeStruct((bsz, MLA_HEADS, seq, HEAD_PAD), BF16),
        grid=(t // tm, MLA_HEADS // heads),
        in_specs=[
            pl.BlockSpec((tm, r), lambda i, j: (i, 0)),
            pl.BlockSpec((r, tn), lambda i, j: (0, j)),
            pl.BlockSpec((tm, LANES), lambda i, j: (i, 0)),
            pl.BlockSpec((tm, LANES), lambda i, j: (i, 0)),
        ],
        out_specs=pl.BlockSpec((None, heads, tm, HEAD_PAD),
                               lambda i, j: (i // tiles_per_batch, j, i % tiles_per_batch, 0)),
        compiler_params=_params(("arbitrary", "arbitrary"), _vmem_limit(blocks, temp_bytes=2 * _nbytes((tm, tn), F32))),
        name="mla_q_up",
    )(cq, w_uq, cos_t, sin_t)


def _kv_up(ckv, w_ukv, kr, bsz, seq, *, tm=1024, heads=4):
    t, r = ckv.shape
    tiles_per_batch = seq // tm
    tn = heads * (MLA_NOPE + MLA_V)
    blocks = [_nbytes((tm, r), BF16), _nbytes((r, tn), BF16), _nbytes((tm, LANES), BF16),
              _nbytes((heads, tm, HEAD_PAD), BF16), _nbytes((heads, tm, MLA_V), BF16)]
    out_map = lambda i, j: (i // tiles_per_batch, j, i % tiles_per_batch, 0)
    vt_map = lambda i, j: (i // tiles_per_batch, j, 0, i % tiles_per_batch)
    return pl.pallas_call(
        functools.partial(_kv_up_kernel, heads=heads),
        out_shape=(jax.ShapeDtypeStruct((bsz, MLA_HEADS, seq, HEAD_PAD), BF16),
                   jax.ShapeDtypeStruct((bsz, MLA_HEADS, MLA_V, seq), BF16)),
        grid=(t // tm, MLA_HEADS // heads),
        in_specs=[
            pl.BlockSpec((tm, r), lambda i, j: (i, 0)),
            pl.BlockSpec((r, tn), lambda i, j: (0, j)),
            pl.BlockSpec((tm, LANES), lambda i, j: (i, 0)),
        ],
        out_specs=(pl.BlockSpec((None, heads, tm, HEAD_PAD), out_map), pl.BlockSpec((None, heads, MLA_V, tm), vt_map)),
        compiler_params=_params(("arbitrary", "arbitrary"), _vmem_limit(blocks, temp_bytes=2 * _nbytes((tm, tn), F32))),
        name="mla_kv_up",
    )(ckv, w_ukv, kr)


def _attn_kernel(q_ref, k_ref, vt_ref, o_ref, m_scr, l_scr, acc_scr, sa_scr, sb_scr, bma_scr, bmb_scr, *, tq, tk, lanes_q):
    qi = pl.program_id(2)
    ngrp = tq // lanes_q
    ndiag = tq // tk
    assert ndiag == 2, "the two-slot score ring below assumes two key blocks per query tile"
    nfull = qi * ndiag
    ring = ((sa_scr, bma_scr), (sb_scr, bmb_scr))
    m_scr[...] = jnp.full(m_scr.shape, -jnp.inf, F32)
    l_scr[...] = jnp.zeros(l_scr.shape, F32)
    acc_scr[...] = jnp.zeros(acc_scr.shape, F32)

    def skipped(c, diag):
        return diag is not None and (c + 1) * lanes_q <= diag * tk

    def scores(blk, diag, slot):
        s_scr, bm_scr = ring[slot]
        kb = k_ref[pl.ds(pl.multiple_of(blk * tk, tk), tk), :]
        for c in range(ngrp):
            if skipped(c, diag):
                continue
            qc = q_ref[c * lanes_q:(c + 1) * lanes_q, :]
            s = lax.dot_general(kb, qc, (((1,), (1,)), ((), ())), preferred_element_type=F32)
            if diag is not None and c * lanes_q < (diag + 1) * tk:
                k_chunk = (lax.broadcasted_iota(jnp.int32, s.shape, 0) + diag * tk) // CHUNK
                q_chunk = (lax.broadcasted_iota(jnp.int32, s.shape, 1) + c * lanes_q) // CHUNK
                s = jnp.where(k_chunk <= q_chunk, s, -jnp.inf)
            s_scr[c] = s
            bm_scr[c] = jnp.max(s, axis=0, keepdims=True)

    def accumulate(blk, diag, slot):
        s_scr, bm_scr = ring[slot]
        vtb = vt_ref[:, pl.ds(pl.multiple_of(blk * tk, tk), tk)]
        for c in range(ngrp):
            if skipped(c, diag):
                continue
            m_prev = m_scr[c]
            m_new = jnp.maximum(m_prev, bm_scr[c])
            alpha = jnp.exp2(m_prev - m_new)
            p = jnp.exp2(s_scr[c] - m_new)
            l_scr[c] = alpha * l_scr[c] + jnp.sum(p, axis=0, keepdims=True)
            acc_scr[c] = alpha * acc_scr[c] + jnp.dot(vtb, p.astype(BF16), preferred_element_type=F32)
            m_scr[c] = m_new

    @pl.when(qi > 0)
    def _():
        scores(0, None, 0)

    @pl.when(qi == 0)
    def _():
        scores(0, 0, 0)

    def body(i, carry):
        b0 = 2 * i
        scores(b0 + 1, None, 1)
        accumulate(b0, None, 0)
        scores(b0 + 2, None, 0)
        accumulate(b0 + 1, None, 1)
        return carry

    lax.fori_loop(0, qi - 1, body, 0)

    @pl.when(qi > 0)
    def _():
        scores(nfull - 1, None, 1)
        accumulate(nfull - 2, None, 0)
        scores(nfull, 0, 0)
        accumulate(nfull - 1, None, 1)

    scores(nfull + 1, 1, 1)
    accumulate(nfull, 0, 0)
    accumulate(nfull + 1, 1, 1)
    for c in range(ngrp):
        o_ref[c * lanes_q:(c + 1) * lanes_q, :] = (acc_scr[c] / l_scr[c]).T.astype(o_ref.dtype)


def _attention(q, k, vt, *, tq=1024, tk=512, lanes_q=256):
    bsz, heads, seq, _ = q.shape
    nq = seq // tq
    ngrp = tq // lanes_q
    blocks = [_nbytes((tq, HEAD_PAD), BF16), _nbytes((seq, HEAD_PAD), BF16), _nbytes((MLA_V, seq), BF16), _nbytes((tq, MLA_V), BF16)]
    scratch = (2 * _nbytes((ngrp, SUBLANES, lanes_q), F32) + _nbytes((ngrp, MLA_V, lanes_q), F32)
               + 2 * _nbytes((ngrp, tk, lanes_q), F32) + 2 * _nbytes((ngrp, SUBLANES, lanes_q), F32))
    return pl.pallas_call(
        functools.partial(_attn_kernel, tq=tq, tk=tk, lanes_q=lanes_q),
        out_shape=jax.ShapeDtypeStruct((bsz * seq, heads * MLA_V), BF16),
        grid=(bsz, heads, nq),
        in_specs=[
            pl.BlockSpec((None, None, tq, HEAD_PAD), lambda b, h, i: (b, h, i, 0)),
            pl.BlockSpec((None, None, seq, HEAD_PAD), lambda b, h, i: (b, h, 0, 0)),
            pl.BlockSpec((None, None, MLA_V, seq), lambda b, h, i: (b, h, 0, 0)),
        ],
        out_specs=pl.BlockSpec((tq, MLA_V), lambda b, h, i: (b * nq + i, h)),
        scratch_shapes=[pltpu.VMEM((ngrp, 1, lanes_q), F32), pltpu.VMEM((ngrp, 1, lanes_q), F32),
                        pltpu.VMEM((ngrp, MLA_V, lanes_q), F32),
                        pltpu.VMEM((ngrp, tk, lanes_q), F32), pltpu.VMEM((ngrp, tk, lanes_q), F32),
                        pltpu.VMEM((ngrp, 1, lanes_q), F32), pltpu.VMEM((ngrp, 1, lanes_q), F32)],
        compiler_params=_params(("arbitrary", "arbitrary", "arbitrary"),
                                _vmem_limit(blocks, scratch, 8 * _nbytes((tk, tq), F32))),
        name="mla_attention",
    )(q, k, vt)


def _causal_conv(t, tail_ref, buf_ref, w_ref, first_of_batch):
    kw = w_ref.shape[0]
    tm = t.shape[0]
    hist = jnp.where(first_of_batch, 0.0, tail_ref[...])
    buf_ref[pl.ds(0, SUBLANES), :] = hist
    buf_ref[pl.ds(SUBLANES, tm), :] = t
    tail_ref[...] = t[tm - SUBLANES:, :]
    acc = t * w_ref[kw - 1:kw, :]
    for k in range(kw - 1):
        acc = acc + buf_ref[pl.ds(SUBLANES - (kw - 1) + k, tm), :] * w_ref[k:k + 1, :]
    return acc


def _conv_in_kernel(x_ref, mod_ref, g_ref, wb_ref, wc_ref, wu_ref, cw_ref, o_ref, h_scr, tail_scr, buf_scr, *, tiles_per_batch):
    i, j = pl.program_id(0), pl.program_id(1)

    @pl.when(j == 0)
    def _():
        h_scr[...] = _modulated_norm(x_ref, mod_ref, g_ref).astype(BF16)

    h = h_scr[...]
    gc = jnp.dot(h, wc_ref[...], preferred_element_type=F32)
    u = jnp.dot(h, wu_ref[...], preferred_element_type=F32)
    conv = _causal_conv(gc * u, tail_scr.at[j], buf_scr, cw_ref, i % tiles_per_batch == 0)
    gb = jnp.dot(h, wb_ref[...], preferred_element_type=F32)
    o_ref[...] = (gb * conv).astype(o_ref.dtype)


def _conv_in(x, mod, g, w_in, conv_w, seq, *, tm=512, tn=512):
    t, d = x.shape
    n = w_in.shape[1] // 3
    nj = n // tn
    tiles_per_batch = seq // tm
    kw = conv_w.shape[0]
    blocks = [_nbytes((tm, d), F32), 3 * _nbytes((d, tn), BF16), _nbytes((tm, tn), BF16)]
    scratch = _nbytes((tm, d), BF16) + _nbytes((nj, SUBLANES, tn), F32) + _nbytes((tm + SUBLANES, tn), F32)
    wspec = lambda off: pl.BlockSpec((d, tn), lambda i, j: (0, j + off * nj))
    return pl.pallas_call(
        functools.partial(_conv_in_kernel, tiles_per_batch=tiles_per_batch),
        out_shape=jax.ShapeDtypeStruct((t, n), BF16),
        grid=(t // tm, nj),
        in_specs=[
            pl.BlockSpec((tm, d), lambda i, j: (i, 0)),
            pl.BlockSpec((None, 3, d), lambda i, j: (i // tiles_per_batch, 0, 0)),
            pl.BlockSpec((1, d), lambda i, j: (0, 0)),
            wspec(0), wspec(1), wspec(2),
            pl.BlockSpec((kw, tn), lambda i, j: (0, j)),
        ],
        out_specs=pl.BlockSpec((tm, tn), lambda i, j: (i, j)),
        scratch_shapes=[pltpu.VMEM((tm, d), BF16), pltpu.VMEM((nj, SUBLANES, tn), F32), pltpu.VMEM((tm + SUBLANES, tn), F32)],
        compiler_params=_params(("arbitrary", "arbitrary"),
                                _vmem_limit(blocks, scratch, _nbytes((tm, d), F32) + 6 * _nbytes((tm, tn), F32))),
        name="conv_in",
    )(x, mod, g.reshape(1, d), w_in, w_in, w_in, conv_w)


def _softplus(x):
    return jnp.maximum(x, 0.0) + jnp.log1p(jnp.exp(-jnp.abs(x)))


def _ssd_in_kernel(x_ref, mod_ref, g_ref, w_ref, wdt_ref, dtb_ref, cw_ref, cb_ref, o_ref, dt_ref,
                   h_scr, tail_scr, buf_scr, *, tiles_per_batch, nz):
    i, j = pl.program_id(0), pl.program_id(1)

    @pl.when(j == 0)
    def _():
        h = _modulated_norm(x_ref, mod_ref, g_ref).astype(BF16)
        h_scr[...] = h
        dt_ref[...] = _softplus(jnp.dot(h, wdt_ref[...], preferred_element_type=F32) + dtb_ref[...])

    y = jnp.dot(h_scr[...], w_ref[...], preferred_element_type=F32)

    @pl.when(j < nz)
    def _():
        o_ref[...] = y.astype(o_ref.dtype)

    @pl.when(j >= nz)
    def _():
        jc = j - nz
        conv = _causal_conv(y, tail_scr.at[jc], buf_scr, cw_ref, i % tiles_per_batch == 0) + cb_ref[...]
        o_ref[...] = _silu(conv).astype(o_ref.dtype)


def _ssd_in(x, mod, g, w_main, w_dt, dt_bias, conv_w, conv_b, d_inner, seq, *, tm=512, tn=512):
    t, d = x.shape
    n = w_main.shape[1]
    nh = w_dt.shape[1]
    nj = n // tn
    nz = d_inner // tn
    nconv = nj - nz
    tiles_per_batch = seq // tm
    kw = conv_w.shape[0]
    blocks = [_nbytes((tm, d), F32), _nbytes((d, tn), BF16), _nbytes((d, nh), BF16), _nbytes((tm, tn), BF16), _nbytes((tm, LANES), F32)]
    scratch = _nbytes((tm, d), BF16) + _nbytes((nconv, SUBLANES, tn), F32) + _nbytes((tm + SUBLANES, tn), F32)
    cmap = lambda i, j: (0, jnp.maximum(j - nz, 0))
    return pl.pallas_call(
        functools.partial(_ssd_in_kernel, tiles_per_batch=tiles_per_batch, nz=nz),
        out_shape=(jax.ShapeDtypeStruct((t, n), BF16), jax.ShapeDtypeStruct((t, nh), F32)),
        grid=(t // tm, nj),
        in_specs=[
            pl.BlockSpec((tm, d), lambda i, j: (i, 0)),
            pl.BlockSpec((None, 3, d), lambda i, j: (i // tiles_per_batch, 0, 0)),
            pl.BlockSpec((1, d), lambda i, j: (0, 0)),
            pl.BlockSpec((d, tn), lambda i, j: (0, j)),
            pl.BlockSpec((d, nh), lambda i, j: (0, 0)),
            pl.BlockSpec((1, nh), lambda i, j: (0, 0)),
            pl.BlockSpec((kw, tn), cmap),
            pl.BlockSpec((1, tn), cmap),
        ],
        out_specs=(pl.BlockSpec((tm, tn), lambda i, j: (i, j)), pl.BlockSpec((tm, nh), lambda i, j: (i, 0))),
        scratch_shapes=[pltpu.VMEM((tm, d), BF16), pltpu.VMEM((nconv, SUBLANES, tn), F32), pltpu.VMEM((tm + SUBLANES, tn), F32)],
        compiler_params=_params(("arbitrary", "arbitrary"),
                                _vmem_limit(blocks, scratch, _nbytes((tm, d), F32) + 6 * _nbytes((tm, tn), F32))),
        name="ssd_in",
    )(x, mod, g.reshape(1, d), w_main, w_dt, dt_bias.reshape(1, nh), conv_w, conv_b.reshape(1, -1))


def _split3(x):
    hi = x.astype(BF16)
    r1 = x - hi.astype(F32)
    mid = r1.astype(BF16)
    lo = (r1 - mid.astype(F32)).astype(BF16)
    return hi, mid, lo


def _dot01(sel, x, *, sel_left):
    out = None
    for part in _split3(x):
        d = (jnp.dot(sel, part, preferred_element_type=F32) if sel_left
             else jnp.dot(part, sel, preferred_element_type=F32))
        out = d if out is None else out + d
    return out


def _ssd_kernel(z_ref, x_ref, b_ref, c_ref, dt_ref, alog_ref, dskip_ref, nw_ref, o_ref, state_scr, *, tt):
    hg = dt_ref.shape[-1]
    gw = x_ref.shape[-1]
    hd = gw // hg
    nch = tt // CHUNK

    @pl.when(pl.program_id(2) == 0)
    def _():
        state_scr[...] = jnp.zeros(state_scr.shape, F32)

    def iota(shape, axis):
        return lax.broadcasted_iota(jnp.int32, shape, axis)

    expand = (iota((hg, gw), 1) // hd == iota((hg, gw), 0)).astype(BF16)
    r2, c2 = iota((tt, tt), 0), iota((tt, tt), 1)
    tri = ((c2 <= r2) & (c2 // CHUNK == r2 // CHUNK)).astype(BF16)
    lrow, lcol = iota((CHUNK, gw), 0), iota((CHUNK, gw), 1) % hd
    half = gw // 2
    bd_keep = iota((half * CHUNK // hd, half), 0) // CHUNK == iota((half * CHUNK // hd, half), 1) // hd

    a_row = -jnp.exp(alog_ref[...])
    dt_all = _dot01(expand, dt_ref[...], sel_left=False)
    dta_all = dt_all * a_row
    acum_all = _dot01(tri, dta_all, sel_left=True)

    for ci in range(nch):
        rows = slice(ci * CHUNK, (ci + 1) * CHUNK)
        x = x_ref[rows, :].astype(F32)
        bm = b_ref[rows, :]
        cm = c_ref[rows, :]
        dt, dta, acum = dt_all[rows], dta_all[rows], acum_all[rows]
        r = jnp.sum(jnp.where(lrow <= lcol, dta, 0.0), axis=0, keepdims=True)
        decay = jnp.exp(jnp.where(lrow >= lcol, acum - r, -jnp.inf))
        cb = lax.dot_general(cm, bm, (((1,), (1,)), ((), ())), preferred_element_type=F32)
        m = (jnp.concatenate([cb] * hg, axis=1) * decay).astype(BF16)
        xdt = x * dt
        xdt16 = xdt.astype(BF16)
        ydiag = []
        for hf in range(2):
            cols = slice(hf * half, (hf + 1) * half)
            blockdiag = jnp.where(bd_keep, jnp.concatenate([xdt16[:, cols]] * (half // hd), axis=0), jnp.zeros((), BF16))
            ydiag.append(jnp.dot(m[:, cols], blockdiag, preferred_element_type=F32))
        last = acum[CHUNK - 1:CHUNK, :]
        xw = (xdt * jnp.exp(last - acum)).astype(BF16)
        st = lax.dot_general(bm, xw, (((0,), (0,)), ((), ())), preferred_element_type=F32)
        prev = state_scr[...]
        yoff = jnp.dot(cm, prev.astype(BF16), preferred_element_type=F32) * jnp.exp(acum)
        state_scr[...] = prev * jnp.exp(last) + st
        y = jnp.concatenate(ydiag, axis=1) + yoff + x * dskip_ref[...]
        gated = y * _silu(z_ref[rows, :].astype(F32))
        o_ref[rows, :] = (_rms(gated) * nw_ref[...]).astype(o_ref.dtype)


def _ssd_scan(zx, dt_g, a_log, d_skip, norm_w, bsz, seq, d_inner, *, tt=256):
    t = zx.shape[0]
    groups, _, hg = dt_g.shape
    gw = d_inner // groups
    n = SSM_STATE
    nt = seq // tt
    zoff, xoff = 0, d_inner // gw
    boff = 2 * d_inner // n
    coff = boff + groups
    row = lambda b, g, i: b * nt + i
    blocks = [2 * _nbytes((tt, gw), BF16), 2 * _nbytes((tt, n), BF16), _nbytes((tt, LANES), F32), _nbytes((tt, gw), BF16)]
    return pl.pallas_call(
        functools.partial(_ssd_kernel, tt=tt),
        out_shape=jax.ShapeDtypeStruct((t, d_inner), BF16),
        grid=(bsz, groups, nt),
        in_specs=[
            pl.BlockSpec((tt, gw), lambda b, g, i: (row(b, g, i), zoff + g)),
            pl.BlockSpec((tt, gw), lambda b, g, i: (row(b, g, i), xoff + g)),
            pl.BlockSpec((tt, n), lambda b, g, i: (row(b, g, i), boff + g)),
            pl.BlockSpec((tt, n), lambda b, g, i: (row(b, g, i), coff + g)),
            pl.BlockSpec((None, tt, hg), lambda b, g, i: (g, row(b, g, i), 0)),
            pl.BlockSpec((None, 1, gw), lambda b, g, i: (g, 0, 0)),
            pl.BlockSpec((None, 1, gw), lambda b, g, i: (g, 0, 0)),
            pl.BlockSpec((1, gw), lambda b, g, i: (0, g)),
        ],
        out_specs=pl.BlockSpec((tt, gw), lambda b, g, i: (row(b, g, i), g)),
        scratch_shapes=[pltpu.VMEM((n, gw), F32)],
        compiler_params=_params(("arbitrary", "arbitrary", "arbitrary"),
                                _vmem_limit(blocks, _nbytes((n, gw), F32), 24 * _nbytes((tt, gw), F32))),
        name="ssd_scan",
    )(zx, zx, zx, zx, dt_g, a_log, d_skip, norm_w.reshape(1, d_inner))


def _pad_heads(w, heads, width, padded):
    r = w.shape[0]
    w = w.reshape(r, heads, width)
    return jnp.pad(w, ((0, 0), (0, 0), (0, padded - width))).reshape(r, heads * padded)


def kernel(x, c, positions, ada_w, ada_b, norm_pre, norm_post, mla_w_down, mla_q_norm, mla_w_uq, mla_kv_norm, mla_w_ukv, mla_w_o, conv_w_in, conv_w, conv_w_out, ssm_w_in, ssm_conv_w, ssm_conv_b, ssm_dt_bias, ssm_a_log, ssm_d, ssm_norm, ssm_w_out, mlp_up, mlp_down):
    bsz, seq, d = x.shape
    depth = ada_w.shape[0]
    t = bsz * seq
    xf = x.reshape(t, d)

    mods = _ada_all(c, ada_w, ada_b).reshape(depth, 2, bsz, 3, d)
    cos_t, sin_t = _rope_tables(positions)

    for i in range(depth):
        kind, j = i % N_MIXERS, i // N_MIXERS
        mod, g_pre, g_post = mods[i, 0], norm_pre[i, 0], norm_post[i, 0]
        if kind == 0:
            w_down = jnp.pad(mla_w_down[j], ((0, 0), (0, LANES - MLA_ROPE))).astype(BF16)
            w_uq = _pad_heads(mla_w_uq[j], MLA_HEADS, MLA_NOPE + MLA_ROPE, HEAD_PAD).astype(BF16)
            cq, ckv, kr = _mla_down(xf, mod, g_pre, w_down, mla_q_norm[j], mla_kv_norm[j], cos_t, sin_t, seq)
            q = _q_up(cq, w_uq, cos_t, sin_t, bsz, seq)
            k, vt = _kv_up(ckv, mla_w_ukv[j].astype(BF16), kr, bsz, seq)
            a = _attention(q, k, vt)
            w_o = mla_w_o[j]
        elif kind == 1:
            a = _conv_in(xf, mod, g_pre, conv_w_in[j].astype(BF16), conv_w[j], seq)
            w_o = conv_w_out[j]
        else:
            d_inner = ssm_w_out.shape[1]
            nh = ssm_dt_bias.shape[1]
            n_main = ssm_w_in.shape[2] - nh
            zx, dt = _ssd_in(xf, mod, g_pre, ssm_w_in[j, :, :n_main].astype(BF16), ssm_w_in[j, :, n_main:].astype(BF16),
                             ssm_dt_bias[j], ssm_conv_w[j], ssm_conv_b[j], d_inner, seq)
            hg = nh // SSM_GROUPS
            gw = d_inner // SSM_GROUPS
            dt_g = dt.reshape(t, SSM_GROUPS, hg).transpose(1, 0, 2)
            per_lane = lambda p: jnp.repeat(p.reshape(SSM_GROUPS, hg), SSM_HEAD_DIM, axis=1).reshape(SSM_GROUPS, 1, gw)
            a = _ssd_scan(zx, dt_g, per_lane(ssm_a_log[j]), per_lane(ssm_d[j]), ssm_norm[j], bsz, seq, d_inner)
            w_o = ssm_w_out[j]
        xf = _mm_resid(a, w_o.astype(BF16), xf, mod, g_post, seq)

        mod, g_pre, g_post = mods[i, 1], norm_pre[i, 1], norm_post[i, 1]
        u = _mm_norm(xf, mod, g_pre, mlp_up[i].astype(BF16), seq, epilogue=_relu2)
        xf = _mm_resid(u, mlp_down[i].astype(BF16), xf, mod, g_post, seq)
    return xf.reshape(bsz, seq, d)
```

```python
import functools
import math

import jax
import jax.numpy as jnp
from jax import lax
from jax.experimental import pallas as pl
from jax.experimental.pallas import tpu as pltpu

F32 = jnp.float32
BF16 = jnp.bfloat16

EPS = 1e-6
CHUNK = 64
N_MIXERS = 3
MLA_HEADS = 16
MLA_LORA = 512
MLA_NOPE = 128
MLA_ROPE = 64
MLA_V = 128
ROPE_THETA = 10000.0
SSM_HEAD_DIM = 64
SSM_GROUPS = 8
SSM_STATE = 128

LANES = 128
SUBLANES = 8
V7X_VMEM_BYTES = 64 * 1024 * 1024
HEAD_PAD = 2 * LANES


def _vmem_limit(block_bytes, scratch_bytes=0, temp_bytes=0):
    need = 2 * sum(block_bytes) + scratch_bytes + temp_bytes + (4 << 20)
    return int(min(need, V7X_VMEM_BYTES - (6 << 20)))


def _nbytes(shape, dtype):
    return math.prod(shape) * jnp.dtype(dtype).itemsize


def _params(sem, limit):
    return pltpu.CompilerParams(dimension_semantics=sem, vmem_limit_bytes=limit)


def _rms(x):
    return x * lax.rsqrt(jnp.mean(x * x, axis=-1, keepdims=True) + EPS)


def _modulated_norm(x, mod_ref, g_ref):
    h = _rms(x) * g_ref[...]
    return h * (1.0 + mod_ref[1:2, :]) + mod_ref[0:1, :]


def _silu(x):
    return x * (1.0 / (1.0 + jnp.exp(-x)))


def _ada_kernel(c_ref, w_ref, b_ref, o_ref):
    c_act = _silu(c_ref[...]).astype(BF16)
    y = jnp.dot(c_act, w_ref[...].astype(BF16), preferred_element_type=F32)
    o_ref[...] = y + b_ref[...]


def _ada_all(c, ada_w, ada_b):
    nsub = ada_w.shape[0] * ada_w.shape[1]
    bsz, d = c.shape
    n = ada_w.shape[-1]
    w = ada_w.reshape(nsub, d, n)
    b = ada_b.reshape(nsub, 1, n)
    tn = 1024
    blocks = [_nbytes((d, tn), F32), _nbytes((bsz, tn), F32)]
    return pl.pallas_call(
        _ada_kernel,
        out_shape=jax.ShapeDtypeStruct((nsub, bsz, n), F32),
        grid=(nsub, n // tn),
        in_specs=[
            pl.BlockSpec((bsz, d), lambda s, j: (0, 0)),
            pl.BlockSpec((None, d, tn), lambda s, j: (s, 0, j)),
            pl.BlockSpec((None, 1, tn), lambda s, j: (s, 0, j)),
        ],
        out_specs=pl.BlockSpec((None, bsz, tn), lambda s, j: (s, 0, j)),
        compiler_params=_params(("arbitrary", "arbitrary"), _vmem_limit(blocks, temp_bytes=_nbytes((d, tn), BF16))),
        name="ada_mod",
    )(c, w, b)


def _rope_kernel(pos_ref, freq_ref, cos_ref, sin_ref):
    ang = pos_ref[...] * freq_ref[...]
    lane = lax.broadcasted_iota(jnp.int32, ang.shape, 1)
    half = MLA_ROPE // 2
    cos_ref[...] = jnp.where(lane < MLA_ROPE, jnp.cos(ang), 0.0)
    s = jnp.sin(ang)
    sin_ref[...] = jnp.where(lane < half, -s, jnp.where(lane < MLA_ROPE, s, 0.0))


def _rope_tables(positions):
    t = positions.size
    half = MLA_ROPE // 2
    inv_freq = ROPE_THETA ** (-jnp.arange(0, MLA_ROPE, 2, dtype=F32) / MLA_ROPE)
    freq = jnp.concatenate([inv_freq, inv_freq, jnp.zeros((LANES - 2 * half,), F32)]).reshape(1, LANES)
    pos = positions.astype(F32).reshape(t, 1)
    tm = min(t, 1024)
    out = jax.ShapeDtypeStruct((t, LANES), F32)
    return pl.pallas_call(
        _rope_kernel,
        out_shape=(out, out),
        grid=(t // tm,),
        in_specs=[pl.BlockSpec((tm, 1), lambda i: (i, 0)), pl.BlockSpec((1, LANES), lambda i: (0, 0))],
        out_specs=(pl.BlockSpec((tm, LANES), lambda i: (i, 0)), pl.BlockSpec((tm, LANES), lambda i: (i, 0))),
        compiler_params=_params(("arbitrary",), _vmem_limit([_nbytes((tm, LANES), F32)] * 3, temp_bytes=8 << 20)),
        name="rope_tables",
    )(pos, freq)


def _rope_slab(z, cos_ref, sin_ref):
    half = MLA_ROPE // 2
    lane = lax.broadcasted_iota(jnp.int32, z.shape, 1)
    partner = jnp.where(lane < half, pltpu.roll(z, LANES - half, axis=1), pltpu.roll(z, half, axis=1))
    return z * cos_ref[...] + partner * sin_ref[...]


def _relu2(y):
    r = jnp.maximum(y, 0.0)
    return r * r


def _norm_mod_kernel(x_ref, mod_ref, g_ref, h_ref):
    h_ref[...] = _modulated_norm(x_ref[...], mod_ref, g_ref).astype(BF16)


def _norm_mod(x, mod, g, seq, *, tm=1024):
    t, d = x.shape
    tiles_per_batch = seq // tm
    blocks = [_nbytes((tm, d), F32), _nbytes((tm, d), BF16)]
    return pl.pallas_call(
        _norm_mod_kernel,
        out_shape=jax.ShapeDtypeStruct((t, d), BF16),
        grid=(t // tm,),
        in_specs=[
            pl.BlockSpec((tm, d), lambda i: (i, 0)),
            pl.BlockSpec((None, 3, d), lambda i: (i // tiles_per_batch, 0, 0)),
            pl.BlockSpec((1, d), lambda i: (0, 0)),
        ],
        out_specs=pl.BlockSpec((tm, d), lambda i: (i, 0)),
        compiler_params=_params(("arbitrary",), _vmem_limit(blocks, temp_bytes=2 * _nbytes((tm, d), F32))),
        name="norm_mod",
    )(x, mod, g.reshape(1, d))


def _mm_w_kernel(h_ref, w_ref, o_ref, w_scr, *, epilogue):
    @pl.when(pl.program_id(1) == 0)
    def _():
        w_scr[...] = w_ref[...].astype(BF16)

    y = jnp.dot(h_ref[...], w_scr[...], preferred_element_type=F32)
    o_ref[...] = epilogue(y).astype(o_ref.dtype)


def _mm_w(h, w_stack, layer, *, epilogue, tm=1024, tn=1024):
    t, d = h.shape
    n = w_stack.shape[2]
    blocks = [_nbytes((tm, d), BF16), _nbytes((d, tn), F32), _nbytes((tm, tn), BF16)]
    return pl.pallas_call(
        functools.partial(_mm_w_kernel, epilogue=epilogue),
        out_shape=jax.ShapeDtypeStruct((t, n), BF16),
        grid=(n // tn, t // tm),
        in_specs=[
            pl.BlockSpec((tm, d), lambda j, i: (i, 0)),
            pl.BlockSpec((None, d, tn), lambda j, i: (layer, 0, j)),
        ],
        out_specs=pl.BlockSpec((tm, tn), lambda j, i: (i, j)),
        scratch_shapes=[pltpu.VMEM((d, tn), BF16)],
        compiler_params=_params(("arbitrary", "arbitrary"),
                                _vmem_limit(blocks, _nbytes((d, tn), BF16), 2 * _nbytes((tm, tn), F32))),
        name="mm_w",
    )(h, w_stack)


_EPILOGUE_ROWS = 256


def _mm_resid_kernel(*refs, nk, tk, resident, emit_next):
    a_ref, w_ref, x_ref, mod_ref, g_ref = refs[:5]
    refs = refs[5:]
    if emit_next:
        modn_ref, gn_ref, o_ref, hn_ref = refs[:4]
        refs = refs[4:]
    else:
        o_ref = refs[0]
        refs = refs[1:]
    i, k = pl.program_id(0), pl.program_id(1)
    if resident:
        w_scr = refs[0]
        rows = pl.ds(pl.multiple_of(k * tk, tk), tk)

        @pl.when(i == 0)
        def _():
            w_scr[rows, :] = w_ref[...].astype(BF16)

        w = w_scr[rows, :]
    else:
        w = w_ref[...]
    @pl.when(k == 0)
    def _():
        o_ref[...] = jnp.zeros(o_ref.shape, F32)

    o_ref[...] += jnp.dot(a_ref[...], w, preferred_element_type=F32)

    @pl.when(k == nk - 1)
    def _():
        for r0 in range(0, o_ref.shape[0], _EPILOGUE_ROWS):
            rows_e = slice(r0, r0 + _EPILOGUE_ROWS)
            x_new = x_ref[rows_e, :] + mod_ref[2:3, :] * (_rms(o_ref[rows_e, :]) * g_ref[...])
            o_ref[rows_e, :] = x_new
            if emit_next:
                hn_ref[rows_e, :] = _modulated_norm(x_new, modn_ref, gn_ref).astype(BF16)


def _mm_resid(a, w_stack, layer, x, mod, g, nxt, seq, *, resident, tm, tk=512):
    t, kdim = a.shape
    d = w_stack.shape[2]
    nk = kdim // tk
    tiles_per_batch = seq // tm
    emit_next = nxt is not None
    batch_map = lambda i, k: (i // tiles_per_batch, 0, 0)
    fixed = lambda i, k: (0, 0)
    w_map = (lambda i, k: (layer, jnp.where(i == 0, k, nk - 1), 0)) if resident else (lambda i, k: (layer, k, 0))
    in_specs = [
        pl.BlockSpec((tm, tk), lambda i, k: (i, k)),
        pl.BlockSpec((None, tk, d), w_map),
        pl.BlockSpec((tm, d), lambda i, k: (i, 0), pipeline_mode=pl.Buffered(1)),
        pl.BlockSpec((None, 3, d), batch_map),
        pl.BlockSpec((1, d), fixed),
    ]
    args = [a, w_stack, x, mod, g.reshape(1, d)]
    out_shape = [jax.ShapeDtypeStruct((t, d), F32)]
    out_specs = [pl.BlockSpec((tm, d), lambda i, k: (i, 0))]
    blocks = [_nbytes((tm, tk), BF16), _nbytes((tk, d), w_stack.dtype), _nbytes((tm, d), F32) // 2, _nbytes((tm, d), F32)]
    if emit_next:
        in_specs += [pl.BlockSpec((None, 3, d), batch_map), pl.BlockSpec((1, d), fixed)]
        args += [nxt[0], nxt[1].reshape(1, d)]
        out_shape.append(jax.ShapeDtypeStruct((t, d), BF16))
        out_specs.append(pl.BlockSpec((tm, d), lambda i, k: (i, 0)))
        blocks.append(_nbytes((tm, d), BF16))
    scratch = [pltpu.VMEM((kdim, d), BF16)] if resident else []
    outs = pl.pallas_call(
        functools.partial(_mm_resid_kernel, nk=nk, tk=tk, resident=resident, emit_next=emit_next),
        out_shape=tuple(out_shape),
        grid=(t // tm, nk),
        in_specs=in_specs,
        out_specs=tuple(out_specs),
        scratch_shapes=scratch,
        compiler_params=_params(("arbitrary", "arbitrary"),
                                _vmem_limit(blocks, _nbytes((kdim, d), BF16) if resident else 0,
                                            _nbytes((tm, d), F32) + 6 * _nbytes((_EPILOGUE_ROWS, d), F32))),
        name="mm_resid",
    )(*args)
    return (outs[0], outs[1]) if emit_next else (outs[0], None)


def _mla_down_kernel(h_ref, w_ref, qn_ref, kvn_ref, cos_ref, sin_ref, cq_ref, ckv_ref, kr_ref, w_scr):
    @pl.when(pl.program_id(0) == 0)
    def _():
        w_scr[...] = w_ref[...].astype(BF16)

    y = jnp.dot(h_ref[...], w_scr[...], preferred_element_type=F32)
    cq_ref[...] = (_rms(y[:, :MLA_LORA]) * qn_ref[...]).astype(BF16)
    ckv_ref[...] = (_rms(y[:, MLA_LORA:2 * MLA_LORA]) * kvn_ref[...]).astype(BF16)
    slab = jnp.concatenate([y[:, 2 * MLA_LORA:], jnp.zeros((y.shape[0], LANES - MLA_ROPE), F32)], axis=1)
    kr_ref[...] = _rope_slab(slab, cos_ref, sin_ref).astype(BF16)


def _mla_down(h, w_stack, layer, q_norm, kv_norm, cos_t, sin_t, *, tm=512):
    t, d = h.shape
    n = w_stack.shape[2]
    blocks = [_nbytes((tm, d), BF16), _nbytes((d, n), F32), 3 * _nbytes((tm, MLA_LORA), BF16), 2 * _nbytes((tm, LANES), F32)]
    row = lambda i: (i, 0)
    fixed = lambda i: (0, 0)
    return pl.pallas_call(
        _mla_down_kernel,
        out_shape=(jax.ShapeDtypeStruct((t, MLA_LORA), BF16), jax.ShapeDtypeStruct((t, MLA_LORA), BF16),
                   jax.ShapeDtypeStruct((t, LANES), BF16)),
        grid=(t // tm,),
        in_specs=[
            pl.BlockSpec((tm, d), row),
            pl.BlockSpec((None, d, n), lambda i: (layer, 0, 0)),
            pl.BlockSpec((1, MLA_LORA), fixed),
            pl.BlockSpec((1, MLA_LORA), fixed),
            pl.BlockSpec((tm, LANES), row),
            pl.BlockSpec((tm, LANES), row),
        ],
        out_specs=(pl.BlockSpec((tm, MLA_LORA), row), pl.BlockSpec((tm, MLA_LORA), row), pl.BlockSpec((tm, LANES), row)),
        scratch_shapes=[pltpu.VMEM((d, n), BF16)],
        compiler_params=_params(("arbitrary",), _vmem_limit(blocks, _nbytes((d, n), BF16), 3 * _nbytes((tm, n), F32))),
        name="mla_down",
    )(h, w_stack, q_norm.reshape(1, -1), kv_norm.reshape(1, -1), cos_t, sin_t)


def _q_up_kernel(a_ref, w_ref, cos_ref, sin_ref, q_ref, *, heads, scale):
    y = jnp.dot(a_ref[...], w_ref[...], preferred_element_type=F32) * scale
    for h in range(heads):
        base = h * HEAD_PAD
        q_ref[h, :, :LANES] = y[:, base:base + LANES].astype(BF16)
        q_ref[h, :, LANES:] = _rope_slab(y[:, base + LANES:base + HEAD_PAD], cos_ref, sin_ref).astype(BF16)


def _kv_up_kernel(a_ref, w_ref, kr_ref, k_ref, vt_ref, *, heads):
    y = jnp.dot(a_ref[...], w_ref[...], preferred_element_type=F32)
    kr = kr_ref[...]
    for h in range(heads):
        base = h * (MLA_NOPE + MLA_V)
        k_ref[h, :, :LANES] = y[:, base:base + MLA_NOPE].astype(BF16)
        k_ref[h, :, LANES:] = kr
        vt_ref[h] = y[:, base + MLA_NOPE:base + MLA_NOPE + MLA_V].T.astype(BF16)


def _q_up(cq, w_uq, cos_t, sin_t, bsz, seq, *, tm=1024, heads=4):
    t, r = cq.shape
    tiles_per_batch = seq // tm
    scale = math.log2(math.e) / math.sqrt(MLA_NOPE + MLA_ROPE)
    tn = heads * HEAD_PAD
    blocks = [_nbytes((tm, r), BF16), _nbytes((r, tn), BF16), 2 * _nbytes((tm, LANES), F32), _nbytes((heads, tm, HEAD_PAD), BF16)]
    return pl.pallas_call(
        functools.partial(_q_up_kernel, heads=heads, scale=scale),
        out_shape=jax.ShapeDtypeStruct((bsz, MLA_HEADS, seq, HEAD_PAD), BF16),
        grid=(t // tm, MLA_HEADS // heads),
        in_specs=[
            pl.BlockSpec((tm, r), lambda i, j: (i, 0)),
            pl.BlockSpec((r, tn), lambda i, j: (0, j)),
            pl.BlockSpec((tm, LANES), lambda i, j: (i, 0)),
            pl.BlockSpec((tm, LANES), lambda i, j: (i, 0)),
        ],
        out_specs=pl.BlockSpec((None, heads, tm, HEAD_PAD),
                               lambda i, j: (i // tiles_per_batch, j, i % tiles_per_batch, 0)),
        compiler_params=_params(("arbitrary", "arbitrary"), _vmem_limit(blocks, temp_bytes=2 * _nbytes((tm, tn), F32))),
        name="mla_q_up",
    )(cq, w_uq, cos_t, sin_t)


def _kv_up(ckv, w_ukv, kr, bsz, seq, *, tm=1024, heads=4):
    t, r = ckv.shape
    tiles_per_batch = seq // tm
    tn = heads * (MLA_NOPE + MLA_V)
    blocks = [_nbytes((tm, r), BF16), _nbytes((r, tn), BF16), _nbytes((tm, LANES), BF16),
              _nbytes((heads, tm, HEAD_PAD), BF16), _nbytes((heads, tm, MLA_V), BF16)]
    out_map = lambda i, j: (i // tiles_per_batch, j, i % tiles_per_batch, 0)
    vt_map = lambda i, j: (i // tiles_per_batch, j, 0, i % tiles_per_batch)
    return pl.pallas_call(
        functools.partial(_kv_up_kernel, heads=heads),
        out_shape=(jax.ShapeDtypeStruct((bsz, MLA_HEADS, seq, HEAD_PAD), BF16),
                   jax.ShapeDtypeStruct((bsz, MLA_HEADS, MLA_V, seq), BF16)),
        grid=(t // tm, MLA_HEADS // heads),
        in_specs=[
            pl.BlockSpec((tm, r), lambda i, j: (i, 0)),
            pl.BlockSpec((r, tn), lambda i, j: (0, j)),
            pl.BlockSpec((tm, LANES), lambda i, j: (i, 0)),
        ],
        out_specs=(pl.BlockSpec((None, heads, tm, HEAD_PAD), out_map), pl.BlockSpec((None, heads, MLA_V, tm), vt_map)),
        compiler_params=_params(("arbitrary", "arbitrary"), _vmem_limit(blocks, temp_bytes=2 * _nbytes((tm, tn), F32))),
        name="mla_kv_up",
    )(ckv, w_ukv, kr)


def _attn_kernel(q_ref, k_ref, vt_ref, o_ref, m_scr, l_scr, acc_scr, sa_scr, sb_scr, bma_scr, bmb_scr, *, tq, tk, lanes_q):
    qi = pl.program_id(2)
    ngrp = tq // lanes_q
    ndiag = tq // tk
    assert ndiag == 2, "the two-slot score ring below assumes two key blocks per query tile"
    nfull = qi * ndiag
    ring = ((sa_scr, bma_scr), (sb_scr, bmb_scr))
    m_scr[...] = jnp.full(m_scr.shape, -jnp.inf, F32)
    l_scr[...] = jnp.zeros(l_scr.shape, F32)
    acc_scr[...] = jnp.zeros(acc_scr.shape, F32)

    def skipped(c, diag):
        return diag is not None and (c + 1) * lanes_q <= diag * tk

    def scores(blk, diag, slot):
        s_scr, bm_scr = ring[slot]
        kb = k_ref[pl.ds(pl.multiple_of(blk * tk, tk), tk), :]
        for c in range(ngrp):
            if skipped(c, diag):
                continue
            qc = q_ref[c * lanes_q:(c + 1) * lanes_q, :]
            s = lax.dot_general(kb, qc, (((1,), (1,)), ((), ())), preferred_element_type=F32)
            if diag is not None and c * lanes_q < (diag + 1) * tk:
                k_chunk = (lax.broadcasted_iota(jnp.int32, s.shape, 0) + diag * tk) // CHUNK
                q_chunk = (lax.broadcasted_iota(jnp.int32, s.shape, 1) + c * lanes_q) // CHUNK
                s = jnp.where(k_chunk <= q_chunk, s, -jnp.inf)
            s_scr[c] = s
            bm_scr[c] = jnp.max(s, axis=0, keepdims=True)

    def accumulate(blk, diag, slot):
        s_scr, bm_scr = ring[slot]
        vtb = vt_ref[:, pl.ds(pl.multiple_of(blk * tk, tk), tk)]
        for c in range(ngrp):
            if skipped(c, diag):
                continue
            m_prev = m_scr[c]
            m_new = jnp.maximum(m_prev, bm_scr[c])
            alpha = jnp.exp2(m_prev - m_new)
            p = jnp.exp2(s_scr[c] - m_new)
            l_scr[c] = alpha * l_scr[c] + jnp.sum(p, axis=0, keepdims=True)
            acc_scr[c] = alpha * acc_scr[c] + jnp.dot(vtb, p.astype(BF16), preferred_element_type=F32)
            m_scr[c] = m_new

    @pl.when(qi > 0)
    def _():
        scores(0, None, 0)

    @pl.when(qi == 0)
    def _():
        scores(0, 0, 0)

    def body(i, carry):
        b0 = 2 * i
        scores(b0 + 1, None, 1)
        accumulate(b0, None, 0)
        scores(b0 + 2, None, 0)
        accumulate(b0 + 1, None, 1)
        return carry

    lax.fori_loop(0, qi - 1, body, 0)

    @pl.when(qi > 0)
    def _():
        scores(nfull - 1, None, 1)
        accumulate(nfull - 2, None, 0)
        scores(nfull, 0, 0)
        accumulate(nfull - 1, None, 1)

    scores(nfull + 1, 1, 1)
    accumulate(nfull, 0, 0)
    accumulate(nfull + 1, 1, 1)
    for c in range(ngrp):
        o_ref[c * lanes_q:(c + 1) * lanes_q, :] = (acc_scr[c] / l_scr[c]).T.astype(o_ref.dtype)


def _attention(q, k, vt, *, tq=1024, tk=512, lanes_q=256):
    bsz, heads, seq, _ = q.shape
    nq = seq // tq
    ngrp = tq // lanes_q
    blocks = [_nbytes((tq, HEAD_PAD), BF16), _nbytes((seq, HEAD_PAD), BF16), _nbytes((MLA_V, seq), BF16), _nbytes((tq, MLA_V), BF16)]
    scratch = (2 * _nbytes((ngrp, SUBLANES, lanes_q), F32) + _nbytes((ngrp, MLA_V, lanes_q), F32)
               + 2 * _nbytes((ngrp, tk, lanes_q), F32) + 2 * _nbytes((ngrp, SUBLANES, lanes_q), F32))
    return pl.pallas_call(
        functools.partial(_attn_kernel, tq=tq, tk=tk, lanes_q=lanes_q),
        out_shape=jax.ShapeDtypeStruct((bsz * seq, heads * MLA_V), BF16),
        grid=(bsz, heads, nq),
        in_specs=[
            pl.BlockSpec((None, None, tq, HEAD_PAD), lambda b, h, i: (b, h, i, 0)),
            pl.BlockSpec((None, None, seq, HEAD_PAD), lambda b, h, i: (b, h, 0, 0)),
            pl.BlockSpec((None, None, MLA_V, seq), lambda b, h, i: (b, h, 0, 0)),
        ],
        out_specs=pl.BlockSpec((tq, MLA_V), lambda b, h, i: (b * nq + i, h)),
        scratch_shapes=[pltpu.VMEM((ngrp, 1, lanes_q), F32), pltpu.VMEM((ngrp, 1, lanes_q), F32),
                        pltpu.VMEM((ngrp, MLA_V, lanes_q), F32),
                        pltpu.VMEM((ngrp, tk, lanes_q), F32), pltpu.VMEM((ngrp, tk, lanes_q), F32),
                        pltpu.VMEM((ngrp, 1, lanes_q), F32), pltpu.VMEM((ngrp, 1, lanes_q), F32)],
        compiler_params=_params(("arbitrary", "arbitrary", "arbitrary"),
                                _vmem_limit(blocks, scratch, 8 * _nbytes((tk, tq), F32))),
        name="mla_attention",
    )(q, k, vt)


def _causal_conv(t, tail_ref, buf_ref, w_ref, first_of_batch):
    kw = w_ref.shape[0]
    tm = t.shape[0]
    hist = jnp.where(first_of_batch, 0.0, tail_ref[...])
    buf_ref[pl.ds(0, SUBLANES), :] = hist
    buf_ref[pl.ds(SUBLANES, tm), :] = t
    tail_ref[...] = t[tm - SUBLANES:, :]
    acc = t * w_ref[kw - 1:kw, :]
    for k in range(kw - 1):
        acc = acc + buf_ref[pl.ds(SUBLANES - (kw - 1) + k, tm), :] * w_ref[k:k + 1, :]
    return acc


def _conv_in_kernel(h_ref, wb_ref, wc_ref, wu_ref, cw_ref, o_ref, wb_scr, wc_scr, wu_scr, tail_scr, buf_scr, *, tiles_per_batch):
    i = pl.program_id(1)

    @pl.when(i == 0)
    def _():
        wb_scr[...] = wb_ref[...].astype(BF16)
        wc_scr[...] = wc_ref[...].astype(BF16)
        wu_scr[...] = wu_ref[...].astype(BF16)

    h = h_ref[...]
    gc = jnp.dot(h, wc_scr[...], preferred_element_type=F32)
    u = jnp.dot(h, wu_scr[...], preferred_element_type=F32)
    conv = _causal_conv(gc * u, tail_scr, buf_scr, cw_ref, i % tiles_per_batch == 0)
    gb = jnp.dot(h, wb_scr[...], preferred_element_type=F32)
    o_ref[...] = (gb * conv).astype(o_ref.dtype)


def _conv_in(h, w_stack, layer, conv_w, seq, *, tm=512, tn=512):
    t, d = h.shape
    n = w_stack.shape[2] // 3
    nj = n // tn
    tiles_per_batch = seq // tm
    kw = conv_w.shape[0]
    blocks = [_nbytes((tm, d), BF16), 3 * _nbytes((d, tn), F32), _nbytes((tm, tn), BF16)]
    scratch = 3 * _nbytes((d, tn), BF16) + _nbytes((SUBLANES, tn), F32) + _nbytes((tm + SUBLANES, tn), F32)
    wspec = lambda off: pl.BlockSpec((None, d, tn), lambda j, i: (layer, 0, j + off * nj))
    return pl.pallas_call(
        functools.partial(_conv_in_kernel, tiles_per_batch=tiles_per_batch),
        out_shape=jax.ShapeDtypeStruct((t, n), BF16),
        grid=(nj, t // tm),
        in_specs=[
            pl.BlockSpec((tm, d), lambda j, i: (i, 0)),
            wspec(0), wspec(1), wspec(2),
            pl.BlockSpec((kw, tn), lambda j, i: (0, j)),
        ],
        out_specs=pl.BlockSpec((tm, tn), lambda j, i: (i, j)),
        scratch_shapes=[pltpu.VMEM((d, tn), BF16), pltpu.VMEM((d, tn), BF16), pltpu.VMEM((d, tn), BF16),
                        pltpu.VMEM((SUBLANES, tn), F32), pltpu.VMEM((tm + SUBLANES, tn), F32)],
        compiler_params=_params(("arbitrary", "arbitrary"), _vmem_limit(blocks, scratch, 6 * _nbytes((tm, tn), F32))),
        name="conv_in",
    )(h, w_stack, w_stack, w_stack, conv_w)


def _softplus(x):
    return jnp.maximum(x, 0.0) + jnp.log1p(jnp.exp(-jnp.abs(x)))


def _ssd_in_kernel(h_ref, w_ref, cw_ref, cb_ref, o_ref, w_scr, tail_scr, buf_scr, *, tiles_per_batch, nz):
    j, i = pl.program_id(0), pl.program_id(1)

    @pl.when(i == 0)
    def _():
        w_scr[...] = w_ref[...].astype(BF16)

    y = jnp.dot(h_ref[...], w_scr[...], preferred_element_type=F32)

    @pl.when(j < nz)
    def _():
        o_ref[...] = y.astype(o_ref.dtype)

    @pl.when(j >= nz)
    def _():
        conv = _causal_conv(y, tail_scr, buf_scr, cw_ref, i % tiles_per_batch == 0) + cb_ref[...]
        o_ref[...] = _silu(conv).astype(o_ref.dtype)


def _ssd_in(h, w_stack, layer, conv_w, conv_b, d_inner, n_main, seq, *, tm=512, tn=1024):
    t, d = h.shape
    nj = n_main // tn
    nz = d_inner // tn
    tiles_per_batch = seq // tm
    kw = conv_w.shape[0]
    blocks = [_nbytes((tm, d), BF16), _nbytes((d, tn), F32), _nbytes((tm, tn), BF16)]
    scratch = _nbytes((d, tn), BF16) + _nbytes((SUBLANES, tn), F32) + _nbytes((tm + SUBLANES, tn), F32)
    cmap = lambda j, i: (0, jnp.maximum(j - nz, 0))
    return pl.pallas_call(
        functools.partial(_ssd_in_kernel, tiles_per_batch=tiles_per_batch, nz=nz),
        out_shape=jax.ShapeDtypeStruct((t, n_main), BF16),
        grid=(nj, t // tm),
        in_specs=[
            pl.BlockSpec((tm, d), lambda j, i: (i, 0)),
            pl.BlockSpec((None, d, tn), lambda j, i: (layer, 0, j)),
            pl.BlockSpec((kw, tn), cmap),
            pl.BlockSpec((1, tn), cmap),
        ],
        out_specs=pl.BlockSpec((tm, tn), lambda j, i: (i, j)),
        scratch_shapes=[pltpu.VMEM((d, tn), BF16), pltpu.VMEM((SUBLANES, tn), F32), pltpu.VMEM((tm + SUBLANES, tn), F32)],
        compiler_params=_params(("arbitrary", "arbitrary"), _vmem_limit(blocks, scratch, 6 * _nbytes((tm, tn), F32))),
        name="ssd_in",
    )(h, w_stack, conv_w, conv_b.reshape(1, -1))


def _ssd_dt_kernel(h_ref, w_ref, b_ref, dt_ref):
    dt_ref[...] = _softplus(jnp.dot(h_ref[...], w_ref[...], preferred_element_type=F32) + b_ref[...])


def _ssd_dt(h, w_dt, dt_bias, *, tm=1024):
    t, d = h.shape
    nh = w_dt.shape[1]
    blocks = [_nbytes((tm, d), BF16), _nbytes((d, LANES), BF16), _nbytes((tm, LANES), F32)]
    return pl.pallas_call(
        _ssd_dt_kernel,
        out_shape=jax.ShapeDtypeStruct((t, nh), F32),
        grid=(t // tm,),
        in_specs=[
            pl.BlockSpec((tm, d), lambda i: (i, 0)),
            pl.BlockSpec((d, nh), lambda i: (0, 0)),
            pl.BlockSpec((1, nh), lambda i: (0, 0)),
        ],
        out_specs=pl.BlockSpec((tm, nh), lambda i: (i, 0)),
        compiler_params=_params(("arbitrary",), _vmem_limit(blocks, temp_bytes=4 * _nbytes((tm, LANES), F32))),
        name="ssd_dt",
    )(h, w_dt, dt_bias.reshape(1, nh))


def _split3(x):
    hi = x.astype(BF16)
    r1 = x - hi.astype(F32)
    mid = r1.astype(BF16)
    lo = (r1 - mid.astype(F32)).astype(BF16)
    return hi, mid, lo


def _dot01(sel, x, *, sel_left):
    out = None
    for part in _split3(x):
        d = (jnp.dot(sel, part, preferred_element_type=F32) if sel_left
             else jnp.dot(part, sel, preferred_element_type=F32))
        out = d if out is None else out + d
    return out


def _ssd_kernel(z_ref, x_ref, b_ref, c_ref, dt_ref, alog_ref, dskip_ref, nw_ref, o_ref, state_scr, *, tt):
    hg = dt_ref.shape[-1]
    gw = x_ref.shape[-1]
    hd = gw // hg
    nch = tt // CHUNK

    @pl.when(pl.program_id(2) == 0)
    def _():
        state_scr[...] = jnp.zeros(state_scr.shape, F32)

    def iota(shape, axis):
        return lax.broadcasted_iota(jnp.int32, shape, axis)

    expand = (iota((hg, gw), 1) // hd == iota((hg, gw), 0)).astype(BF16)
    r2, c2 = iota((tt, tt), 0), iota((tt, tt), 1)
    tri = ((c2 <= r2) & (c2 // CHUNK == r2 // CHUNK)).astype(BF16)
    lrow, lcol = iota((CHUNK, gw), 0), iota((CHUNK, gw), 1) % hd
    half = gw // 2
    bd_keep = iota((half * CHUNK // hd, half), 0) // CHUNK == iota((half * CHUNK // hd, half), 1) // hd

    a_row = -jnp.exp(alog_ref[...])
    dt_all = _dot01(expand, dt_ref[...], sel_left=False)
    dta_all = dt_all * a_row
    acum_all = _dot01(tri, dta_all, sel_left=True)

    for ci in range(nch):
        rows = slice(ci * CHUNK, (ci + 1) * CHUNK)
        x = x_ref[rows, :].astype(F32)
        bm = b_ref[rows, :]
        cm = c_ref[rows, :]
        dt, dta, acum = dt_all[rows], dta_all[rows], acum_all[rows]
        r = jnp.sum(jnp.where(lrow <= lcol, dta, 0.0), axis=0, keepdims=True)
        decay = jnp.exp(jnp.where(lrow >= lcol, acum - r, -jnp.inf))
        cb = lax.dot_general(cm, bm, (((1,), (1,)), ((), ())), preferred_element_type=F32)
        m = (jnp.concatenate([cb] * hg, axis=1) * decay).astype(BF16)
        xdt = x * dt
        xdt16 = xdt.astype(BF16)
        ydiag = []
        for hf in range(2):
            cols = slice(hf * half, (hf + 1) * half)
            blockdiag = jnp.where(bd_keep, jnp.concatenate([xdt16[:, cols]] * (half // hd), axis=0), jnp.zeros((), BF16))
            ydiag.append(jnp.dot(m[:, cols], blockdiag, preferred_element_type=F32))
        last = acum[CHUNK - 1:CHUNK, :]
        xw = (xdt * jnp.exp(last - acum)).astype(BF16)
        st = lax.dot_general(bm, xw, (((0,), (0,)), ((), ())), preferred_element_type=F32)
        prev = state_scr[...]
        yoff = jnp.dot(cm, prev.astype(BF16), preferred_element_type=F32) * jnp.exp(acum)
        state_scr[...] = prev * jnp.exp(last) + st
        y = jnp.concatenate(ydiag, axis=1) + yoff + x * dskip_ref[...]
        gated = y * _silu(z_ref[rows, :].astype(F32))
        o_ref[rows, :] = (_rms(gated) * nw_ref[...]).astype(o_ref.dtype)


def _ssd_scan(zx, dt_g, a_log, d_skip, norm_w, bsz, seq, d_inner, *, tt=256):
    t = zx.shape[0]
    groups, _, hg = dt_g.shape
    gw = d_inner // groups
    n = SSM_STATE
    nt = seq // tt
    zoff, xoff = 0, d_inner // gw
    boff = 2 * d_inner // n
    coff = boff + groups
    row = lambda b, g, i: b * nt + i
    blocks = [2 * _nbytes((tt, gw), BF16), 2 * _nbytes((tt, n), BF16), _nbytes((tt, LANES), F32), _nbytes((tt, gw), BF16)]
    return pl.pallas_call(
        functools.partial(_ssd_kernel, tt=tt),
        out_shape=jax.ShapeDtypeStruct((t, d_inner), BF16),
        grid=(bsz, groups, nt),
        in_specs=[
            pl.BlockSpec((tt, gw), lambda b, g, i: (row(b, g, i), zoff + g)),
            pl.BlockSpec((tt, gw), lambda b, g, i: (row(b, g, i), xoff + g)),
            pl.BlockSpec((tt, n), lambda b, g, i: (row(b, g, i), boff + g)),
            pl.BlockSpec((tt, n), lambda b, g, i: (row(b, g, i), coff + g)),
            pl.BlockSpec((None, tt, hg), lambda b, g, i: (g, row(b, g, i), 0)),
            pl.BlockSpec((None, 1, gw), lambda b, g, i: (g, 0, 0)),
            pl.BlockSpec((None, 1, gw), lambda b, g, i: (g, 0, 0)),
            pl.BlockSpec((1, gw), lambda b, g, i: (0, g)),
        ],
        out_specs=pl.BlockSpec((tt, gw), lambda b, g, i: (row(b, g, i), g)),
        scratch_shapes=[pltpu.VMEM((n, gw), F32)],
        compiler_params=_params(("arbitrary", "arbitrary", "arbitrary"),
                                _vmem_limit(blocks, _nbytes((n, gw), F32), 24 * _nbytes((tt, gw), F32))),
        name="ssd_scan",
    )(zx, zx, zx, zx, dt_g, a_log, d_skip, norm_w.reshape(1, d_inner))


def _pad_heads(w, heads, width, padded):
    r = w.shape[0]
    w = w.reshape(r, heads, width)
    return jnp.pad(w, ((0, 0), (0, 0), (0, padded - width))).reshape(r, heads * padded)


def kernel(x, c, positions, ada_w, ada_b, norm_pre, norm_post, mla_w_down, mla_q_norm, mla_w_uq, mla_kv_norm, mla_w_ukv, mla_w_o, conv_w_in, conv_w, conv_w_out, ssm_w_in, ssm_conv_w, ssm_conv_b, ssm_dt_bias, ssm_a_log, ssm_d, ssm_norm, ssm_w_out, mlp_up, mlp_down):
    bsz, seq, d = x.shape
    depth = ada_w.shape[0]
    t = bsz * seq
    xf = x.reshape(t, d)

    mods = _ada_all(c, ada_w, ada_b).reshape(depth, 2, bsz, 3, d)
    cos_t, sin_t = _rope_tables(positions)
    mlp_down16 = mlp_down.astype(BF16)

    h = _norm_mod(xf, mods[0, 0], norm_pre[0, 0], seq)
    for i in range(depth):
        kind, j = i % N_MIXERS, i // N_MIXERS
        if kind == 0:
            w_uq = _pad_heads(mla_w_uq[j], MLA_HEADS, MLA_NOPE + MLA_ROPE, HEAD_PAD).astype(BF16)
            cq, ckv, kr = _mla_down(h, mla_w_down, j, mla_q_norm[j], mla_kv_norm[j], cos_t, sin_t)
            q = _q_up(cq, w_uq, cos_t, sin_t, bsz, seq)
            k, vt = _kv_up(ckv, mla_w_ukv[j].astype(BF16), kr, bsz, seq)
            a = _attention(q, k, vt)
            w_o = mla_w_o
        elif kind == 1:
            a = _conv_in(h, conv_w_in, j, conv_w[j], seq)
            w_o = conv_w_out
        else:
            d_inner = ssm_w_out.shape[1]
            nh = ssm_dt_bias.shape[1]
            n_main = ssm_w_in.shape[2] - nh
            zx = _ssd_in(h, ssm_w_in, j, ssm_conv_w[j], ssm_conv_b[j], d_inner, n_main, seq)
            dt = _ssd_dt(h, ssm_w_in[j, :, n_main:].astype(BF16), ssm_dt_bias[j])
            hg = nh // SSM_GROUPS
            gw = d_inner // SSM_GROUPS
            dt_g = dt.reshape(t, SSM_GROUPS, hg).transpose(1, 0, 2)
            per_lane = lambda p: jnp.repeat(p.reshape(SSM_GROUPS, hg), SSM_HEAD_DIM, axis=1).reshape(SSM_GROUPS, 1, gw)
            a = _ssd_scan(zx, dt_g, per_lane(ssm_a_log[j]), per_lane(ssm_d[j]), ssm_norm[j], bsz, seq, d_inner)
            w_o = ssm_w_out
        xf, h = _mm_resid(a, w_o, j, xf, mods[i, 0], norm_post[i, 0], (mods[i, 1], norm_pre[i, 1]), seq,
                          resident=True, tm=512)
        u = _mm_w(h, mlp_up, i, epilogue=_relu2)
        nxt = (mods[i + 1, 0], norm_pre[i + 1, 0]) if i + 1 < depth else None
        xf, h = _mm_resid(u, mlp_down16, i, xf, mods[i, 1], norm_post[i, 1], nxt, seq, resident=False, tm=1024)
    return xf.reshape(bsz, seq, d)
```

```python
import functools
import math

import jax
import jax.numpy as jnp
from jax import lax
from jax.experimental import pallas as pl
from jax.experimental.pallas import tpu as pltpu

F32 = jnp.float32
BF16 = jnp.bfloat16

EPS = 1e-6
CHUNK = 64
N_MIXERS = 3
MLA_HEADS = 16
MLA_LORA = 512
MLA_NOPE = 128
MLA_ROPE = 64
MLA_V = 128
ROPE_THETA = 10000.0
SSM_HEAD_DIM = 64
SSM_GROUPS = 8
SSM_STATE = 128

LANES = 128
SUBLANES = 8
V7X_VMEM_BYTES = 64 * 1024 * 1024
HEAD_PAD = 2 * LANES


def _vmem_limit(block_bytes, scratch_bytes=0, temp_bytes=0):
    need = 2 * sum(block_bytes) + scratch_bytes + temp_bytes + (4 << 20)
    return int(min(need, V7X_VMEM_BYTES - (6 << 20)))


def _nbytes(shape, dtype):
    return math.prod(shape) * jnp.dtype(dtype).itemsize


def _params(sem, limit):
    return pltpu.CompilerParams(dimension_semantics=sem, vmem_limit_bytes=limit)


def _rms(x):
    return x * lax.rsqrt(jnp.mean(x * x, axis=-1, keepdims=True) + EPS)


def _modulated_norm(x, mod_ref, g_ref):
    h = _rms(x) * g_ref[...]
    return h * (1.0 + mod_ref[1:2, :]) + mod_ref[0:1, :]


def _silu(x):
    return x * (1.0 / (1.0 + jnp.exp(-x)))


def _ada_kernel(c_ref, w_ref, b_ref, o_ref):
    c_act = _silu(c_ref[...]).astype(BF16)
    y = jnp.dot(c_act, w_ref[...].astype(BF16), preferred_element_type=F32)
    o_ref[...] = y + b_ref[...]


def _ada_all(c, ada_w, ada_b):
    nsub = ada_w.shape[0] * ada_w.shape[1]
    bsz, d = c.shape
    n = ada_w.shape[-1]
    w = ada_w.reshape(nsub, d, n)
    b = ada_b.reshape(nsub, 1, n)
    tn = 1024
    blocks = [_nbytes((d, tn), F32), _nbytes((bsz, tn), F32)]
    return pl.pallas_call(
        _ada_kernel,
        out_shape=jax.ShapeDtypeStruct((nsub, bsz, n), F32),
        grid=(nsub, n // tn),
        in_specs=[
            pl.BlockSpec((bsz, d), lambda s, j: (0, 0)),
            pl.BlockSpec((None, d, tn), lambda s, j: (s, 0, j)),
            pl.BlockSpec((None, 1, tn), lambda s, j: (s, 0, j)),
        ],
        out_specs=pl.BlockSpec((None, bsz, tn), lambda s, j: (s, 0, j)),
        compiler_params=_params(("arbitrary", "arbitrary"), _vmem_limit(blocks, temp_bytes=_nbytes((d, tn), BF16))),
        name="ada_mod",
    )(c, w, b)


def _rope_kernel(pos_ref, freq_ref, cos_ref, sin_ref):
    ang = pos_ref[...] * freq_ref[...]
    lane = lax.broadcasted_iota(jnp.int32, ang.shape, 1)
    half = MLA_ROPE // 2
    cos_ref[...] = jnp.where(lane < MLA_ROPE, jnp.cos(ang), 0.0)
    s = jnp.sin(ang)
    sin_ref[...] = jnp.where(lane < half, -s, jnp.where(lane < MLA_ROPE, s, 0.0))


def _rope_tables(positions):
    t = positions.size
    half = MLA_ROPE // 2
    inv_freq = ROPE_THETA ** (-jnp.arange(0, MLA_ROPE, 2, dtype=F32) / MLA_ROPE)
    freq = jnp.concatenate([inv_freq, inv_freq, jnp.zeros((LANES - 2 * half,), F32)]).reshape(1, LANES)
    pos = positions.astype(F32).reshape(t, 1)
    tm = min(t, 1024)
    out = jax.ShapeDtypeStruct((t, LANES), F32)
    return pl.pallas_call(
        _rope_kernel,
        out_shape=(out, out),
        grid=(t // tm,),
        in_specs=[pl.BlockSpec((tm, 1), lambda i: (i, 0)), pl.BlockSpec((1, LANES), lambda i: (0, 0))],
        out_specs=(pl.BlockSpec((tm, LANES), lambda i: (i, 0)), pl.BlockSpec((tm, LANES), lambda i: (i, 0))),
        compiler_params=_params(("arbitrary",), _vmem_limit([_nbytes((tm, LANES), F32)] * 3, temp_bytes=8 << 20)),
        name="rope_tables",
    )(pos, freq)


def _rope_slab(z, cos_ref, sin_ref):
    half = MLA_ROPE // 2
    lane = lax.broadcasted_iota(jnp.int32, z.shape, 1)
    partner = jnp.where(lane < half, pltpu.roll(z, LANES - half, axis=1), pltpu.roll(z, half, axis=1))
    return z * cos_ref[...] + partner * sin_ref[...]


def _relu2(y):
    r = jnp.maximum(y, 0.0)
    return r * r


def _norm_mod_kernel(x_ref, mod_ref, g_ref, h_ref):
    h_ref[...] = _modulated_norm(x_ref[...], mod_ref, g_ref).astype(BF16)


def _norm_mod(x, mod, g, seq, *, tm=1024):
    t, d = x.shape
    tiles_per_batch = seq // tm
    blocks = [_nbytes((tm, d), F32), _nbytes((tm, d), BF16)]
    return pl.pallas_call(
        _norm_mod_kernel,
        out_shape=jax.ShapeDtypeStruct((t, d), BF16),
        grid=(t // tm,),
        in_specs=[
            pl.BlockSpec((tm, d), lambda i: (i, 0)),
            pl.BlockSpec((None, 3, d), lambda i: (i // tiles_per_batch, 0, 0)),
            pl.BlockSpec((1, d), lambda i: (0, 0)),
        ],
        out_specs=pl.BlockSpec((tm, d), lambda i: (i, 0)),
        compiler_params=_params(("arbitrary",), _vmem_limit(blocks, temp_bytes=2 * _nbytes((tm, d), F32))),
        name="norm_mod",
    )(x, mod, g.reshape(1, d))


def _mm_w_kernel(h_ref, w_ref, o_ref, w_scr, *, epilogue):
    @pl.when(pl.program_id(1) == 0)
    def _():
        w_scr[...] = w_ref[...].astype(BF16)

    y = jnp.dot(h_ref[...], w_scr[...], preferred_element_type=F32)
    o_ref[...] = epilogue(y).astype(o_ref.dtype)


def _identity(y):
    return y


def _mm_w(h, w_stack, layer, *, epilogue, ncols=None, tm=1024, tn=1024):
    t, d = h.shape
    n = w_stack.shape[2] if ncols is None else ncols
    blocks = [_nbytes((tm, d), BF16), _nbytes((d, tn), F32), _nbytes((tm, tn), BF16)]
    return pl.pallas_call(
        functools.partial(_mm_w_kernel, epilogue=epilogue),
        out_shape=jax.ShapeDtypeStruct((t, n), BF16),
        grid=(n // tn, t // tm),
        in_specs=[
            pl.BlockSpec((tm, d), lambda j, i: (i, 0)),
            pl.BlockSpec((None, d, tn), lambda j, i: (layer, 0, j)),
        ],
        out_specs=pl.BlockSpec((tm, tn), lambda j, i: (i, j)),
        scratch_shapes=[pltpu.VMEM((d, tn), BF16)],
        compiler_params=_params(("arbitrary", "arbitrary"),
                                _vmem_limit(blocks, _nbytes((d, tn), BF16), 2 * _nbytes((tm, tn), F32))),
        name="mm_w",
    )(h, w_stack)


_EPILOGUE_ROWS = 256


def _mm_resid_kernel(*refs, nk, tk, resident, emit_next):
    a_ref, w_ref, x_ref, mod_ref, g_ref = refs[:5]
    refs = refs[5:]
    if emit_next:
        modn_ref, gn_ref, o_ref, hn_ref = refs[:4]
        refs = refs[4:]
    else:
        o_ref = refs[0]
        refs = refs[1:]
    i, k = pl.program_id(0), pl.program_id(1)
    x_scr, x_sem = refs[-2:]
    tm = o_ref.shape[0]

    def x_copy():
        return pltpu.make_async_copy(x_ref.at[pl.ds(pl.multiple_of(i * tm, tm), tm), :], x_scr, x_sem)

    @pl.when(k == 0)
    def _():
        x_copy().start()

    if resident:
        w_scr = refs[0]
        rows = pl.ds(pl.multiple_of(k * tk, tk), tk)

        @pl.when(i == 0)
        def _():
            w_scr[rows, :] = w_ref[...].astype(BF16)

        w = w_scr[rows, :]
    else:
        w = w_ref[...]
    @pl.when(k == 0)
    def _():
        o_ref[...] = jnp.zeros(o_ref.shape, F32)

    o_ref[...] += jnp.dot(a_ref[...], w, preferred_element_type=F32)

    @pl.when(k == nk - 1)
    def _():
        x_copy().wait()
        gate_g = mod_ref[2:3, :] * g_ref[...]
        if emit_next:
            scale_g = gn_ref[...] * (1.0 + modn_ref[1:2, :])
        for r0 in range(0, tm, _EPILOGUE_ROWS):
            rows_e = slice(r0, r0 + _EPILOGUE_ROWS)
            x_new = x_scr[rows_e, :] + _rms(o_ref[rows_e, :]) * gate_g
            o_ref[rows_e, :] = x_new
            if emit_next:
                hn_ref[rows_e, :] = (_rms(x_new) * scale_g + modn_ref[0:1, :]).astype(BF16)


def _mm_resid(a, w_stack, layer, x, mod, g, nxt, seq, *, resident, tm, tk=512):
    t, kdim = a.shape
    d = w_stack.shape[2]
    nk = kdim // tk
    tiles_per_batch = seq // tm
    emit_next = nxt is not None
    batch_map = lambda i, k: (i // tiles_per_batch, 0, 0)
    fixed = lambda i, k: (0, 0)
    w_map = (lambda i, k: (layer, jnp.where(i == 0, k, nk - 1), 0)) if resident else (lambda i, k: (layer, k, 0))
    in_specs = [
        pl.BlockSpec((tm, tk), lambda i, k: (i, k)),
        pl.BlockSpec((None, tk, d), w_map),
        pl.BlockSpec(memory_space=pl.ANY),
        pl.BlockSpec((None, 3, d), batch_map),
        pl.BlockSpec((1, d), fixed),
    ]
    args = [a, w_stack, x, mod, g.reshape(1, d)]
    out_shape = [jax.ShapeDtypeStruct((t, d), F32)]
    out_specs = [pl.BlockSpec((tm, d), lambda i, k: (i, 0))]
    blocks = [_nbytes((tm, tk), BF16), _nbytes((tk, d), w_stack.dtype), _nbytes((tm, d), F32)]
    if emit_next:
        in_specs += [pl.BlockSpec((None, 3, d), batch_map), pl.BlockSpec((1, d), fixed)]
        args += [nxt[0], nxt[1].reshape(1, d)]
        out_shape.append(jax.ShapeDtypeStruct((t, d), BF16))
        out_specs.append(pl.BlockSpec((tm, d), lambda i, k: (i, 0)))
        blocks.append(_nbytes((tm, d), BF16))
    scratch = [pltpu.VMEM((kdim, d), BF16)] if resident else []
    scratch += [pltpu.VMEM((tm, d), F32), pltpu.SemaphoreType.DMA(())]
    outs = pl.pallas_call(
        functools.partial(_mm_resid_kernel, nk=nk, tk=tk, resident=resident, emit_next=emit_next),
        out_shape=tuple(out_shape),
        grid=(t // tm, nk),
        in_specs=in_specs,
        out_specs=tuple(out_specs),
        scratch_shapes=scratch,
        compiler_params=_params(("arbitrary", "arbitrary"),
                                _vmem_limit(blocks, _nbytes((tm, d), F32) + (_nbytes((kdim, d), BF16) if resident else 0),
                                            _nbytes((tm, d), F32) + 6 * _nbytes((_EPILOGUE_ROWS, d), F32))),
        name="mm_resid",
    )(*args)
    return (outs[0], outs[1]) if emit_next else (outs[0], None)


def _mla_down_kernel(h_ref, w_ref, qn_ref, kvn_ref, cos_ref, sin_ref, cq_ref, ckv_ref, kr_ref, w_scr):
    @pl.when(pl.program_id(0) == 0)
    def _():
        w_scr[...] = w_ref[...].astype(BF16)

    y = jnp.dot(h_ref[...], w_scr[...], preferred_element_type=F32)
    cq_ref[...] = (_rms(y[:, :MLA_LORA]) * qn_ref[...]).astype(BF16)
    ckv_ref[...] = (_rms(y[:, MLA_LORA:2 * MLA_LORA]) * kvn_ref[...]).astype(BF16)
    slab = jnp.concatenate([y[:, 2 * MLA_LORA:], jnp.zeros((y.shape[0], LANES - MLA_ROPE), F32)], axis=1)
    kr_ref[...] = _rope_slab(slab, cos_ref, sin_ref).astype(BF16)


def _mla_down(h, w_stack, layer, q_norm, kv_norm, cos_t, sin_t, *, tm=512):
    t, d = h.shape
    n = w_stack.shape[2]
    blocks = [_nbytes((tm, d), BF16), _nbytes((d, n), F32), 3 * _nbytes((tm, MLA_LORA), BF16), 2 * _nbytes((tm, LANES), F32)]
    row = lambda i: (i, 0)
    fixed = lambda i: (0, 0)
    return pl.pallas_call(
        _mla_down_kernel,
        out_shape=(jax.ShapeDtypeStruct((t, MLA_LORA), BF16), jax.ShapeDtypeStruct((t, MLA_LORA), BF16),
                   jax.ShapeDtypeStruct((t, LANES), BF16)),
        grid=(t // tm,),
        in_specs=[
            pl.BlockSpec((tm, d), row),
            pl.BlockSpec((None, d, n), lambda i: (layer, 0, 0)),
            pl.BlockSpec((1, MLA_LORA), fixed),
            pl.BlockSpec((1, MLA_LORA), fixed),
            pl.BlockSpec((tm, LANES), row),
            pl.BlockSpec((tm, LANES), row),
        ],
        out_specs=(pl.BlockSpec((tm, MLA_LORA), row), pl.BlockSpec((tm, MLA_LORA), row), pl.BlockSpec((tm, LANES), row)),
        scratch_shapes=[pltpu.VMEM((d, n), BF16)],
        compiler_params=_params(("arbitrary",), _vmem_limit(blocks, _nbytes((d, n), BF16), 3 * _nbytes((tm, n), F32))),
        name="mla_down",
    )(h, w_stack, q_norm.reshape(1, -1), kv_norm.reshape(1, -1), cos_t, sin_t)


def _q_up_kernel(a_ref, w_ref, cos_ref, sin_ref, q_ref, *, heads, scale):
    y = jnp.dot(a_ref[...], w_ref[...], preferred_element_type=F32) * scale
    for h in range(heads):
        base = h * HEAD_PAD
        q_ref[h, :, :LANES] = y[:, base:base + LANES].astype(BF16)
        q_ref[h, :, LANES:] = _rope_slab(y[:, base + LANES:base + HEAD_PAD], cos_ref, sin_ref).astype(BF16)


def _kv_up_kernel(a_ref, w_ref, kr_ref, k_ref, vt_ref, *, heads):
    y = jnp.dot(a_ref[...], w_ref[...], preferred_element_type=F32)
    kr = kr_ref[...]
    for h in range(heads):
        base = h * (MLA_NOPE + MLA_V)
        k_ref[h, :, :LANES] = y[:, base:base + MLA_NOPE].astype(BF16)
        k_ref[h, :, LANES:] = kr
        vt_ref[h] = y[:, base + MLA_NOPE:base + MLA_NOPE + MLA_V].T.astype(BF16)


def _q_up(cq, w_uq, cos_t, sin_t, bsz, seq, *, tm=1024, heads=4):
    t, r = cq.shape
    tiles_per_batch = seq // tm
    scale = math.log2(math.e) / math.sqrt(MLA_NOPE + MLA_ROPE)
    tn = heads * HEAD_PAD
    blocks = [_nbytes((tm, r), BF16), _nbytes((r, tn), BF16), 2 * _nbytes((tm, LANES), F32), _nbytes((heads, tm, HEAD_PAD), BF16)]
    return pl.pallas_call(
        functools.partial(_q_up_kernel, heads=heads, scale=scale),
        out_shape=jax.ShapeDtypeStruct((bsz, MLA_HEADS, seq, HEAD_PAD), BF16),
        grid=(t // tm, MLA_HEADS // heads),
        in_specs=[
            pl.BlockSpec((tm, r), lambda i, j: (i, 0)),
            pl.BlockSpec((r, tn), lambda i, j: (0, j)),
            pl.BlockSpec((tm, LANES), lambda i, j: (i, 0)),
            pl.BlockSpec((tm, LANES), lambda i, j: (i, 0)),
        ],
        out_specs=pl.BlockSpec((None, heads, tm, HEAD_PAD),
                               lambda i, j: (i // tiles_per_batch, j, i % tiles_per_batch, 0)),
        compiler_params=_params(("arbitrary", "arbitrary"), _vmem_limit(blocks, temp_bytes=2 * _nbytes((tm, tn), F32))),
        name="mla_q_up",
    )(cq, w_uq, cos_t, sin_t)


def _kv_up(ckv, w_ukv, kr, bsz, seq, *, tm=1024, heads=4):
    t, r = ckv.shape
    tiles_per_batch = seq // tm
    tn = heads * (MLA_NOPE + MLA_V)
    blocks = [_nbytes((tm, r), BF16), _nbytes((r, tn), BF16), _nbytes((tm, LANES), BF16),
              _nbytes((heads, tm, HEAD_PAD), BF16), _nbytes((heads, tm, MLA_V), BF16)]
    out_map = lambda i, j: (i // tiles_per_batch, j, i % tiles_per_batch, 0)
    vt_map = lambda i, j: (i // tiles_per_batch, j, 0, i % tiles_per_batch)
    return pl.pallas_call(
        functools.partial(_kv_up_kernel, heads=heads),
        out_shape=(jax.ShapeDtypeStruct((bsz, MLA_HEADS, seq, HEAD_PAD), BF16),
                   jax.ShapeDtypeStruct((bsz, MLA_HEADS, MLA_V, seq), BF16)),
        grid=(t // tm, MLA_HEADS // heads),
        in_specs=[
            pl.BlockSpec((tm, r), lambda i, j: (i, 0)),
            pl.BlockSpec((r, tn), lambda i, j: (0, j)),
            pl.BlockSpec((tm, LANES), lambda i, j: (i, 0)),
        ],
        out_specs=(pl.BlockSpec((None, heads, tm, HEAD_PAD), out_map), pl.BlockSpec((None, heads, MLA_V, tm), vt_map)),
        compiler_params=_params(("arbitrary", "arbitrary"), _vmem_limit(blocks, temp_bytes=2 * _nbytes((tm, tn), F32))),
        name="mla_kv_up",
    )(ckv, w_ukv, kr)


def _attn_kernel(q_ref, k_ref, vt_ref, o_ref, m_scr, l_scr, acc_scr, sa_scr, sb_scr, bma_scr, bmb_scr, *, tq, tk, lanes_q):
    qi = pl.program_id(2)
    ngrp = tq // lanes_q
    ndiag = tq // tk
    assert ndiag == 2, "the two-slot score ring below assumes two key blocks per query tile"
    nfull = qi * ndiag
    ring = ((sa_scr, bma_scr), (sb_scr, bmb_scr))
    m_scr[...] = jnp.full(m_scr.shape, -jnp.inf, F32)
    l_scr[...] = jnp.zeros(l_scr.shape, F32)
    acc_scr[...] = jnp.zeros(acc_scr.shape, F32)

    def skipped(c, diag):
        return diag is not None and (c + 1) * lanes_q <= diag * tk

    def scores(blk, diag, slot):
        s_scr, bm_scr = ring[slot]
        kb = k_ref[pl.ds(pl.multiple_of(blk * tk, tk), tk), :]
        for c in range(ngrp):
            if skipped(c, diag):
                continue
            qc = q_ref[c * lanes_q:(c + 1) * lanes_q, :]
            s = lax.dot_general(kb, qc, (((1,), (1,)), ((), ())), preferred_element_type=F32)
            if diag is not None and c * lanes_q < (diag + 1) * tk:
                k_chunk = (lax.broadcasted_iota(jnp.int32, s.shape, 0) + diag * tk) // CHUNK
                q_chunk = (lax.broadcasted_iota(jnp.int32, s.shape, 1) + c * lanes_q) // CHUNK
                s = jnp.where(k_chunk <= q_chunk, s, -jnp.inf)
            s_scr[c] = s
            bm_scr[c] = jnp.max(s, axis=0, keepdims=True)

    def accumulate(blk, diag, slot):
        s_scr, bm_scr = ring[slot]
        vtb = vt_ref[:, pl.ds(pl.multiple_of(blk * tk, tk), tk)]
        for c in range(ngrp):
            if skipped(c, diag):
                continue
            m_prev = m_scr[c]
            m_new = jnp.maximum(m_prev, bm_scr[c])
            alpha = jnp.exp2(m_prev - m_new)
            p = jnp.exp2(s_scr[c] - m_new)
            l_scr[c] = alpha * l_scr[c] + jnp.sum(p, axis=0, keepdims=True)
            acc_scr[c] = alpha * acc_scr[c] + jnp.dot(vtb, p.astype(BF16), preferred_element_type=F32)
            m_scr[c] = m_new

    @pl.when(qi > 0)
    def _():
        scores(0, None, 0)

    @pl.when(qi == 0)
    def _():
        scores(0, 0, 0)

    def body(i, carry):
        b0 = 2 * i
        scores(b0 + 1, None, 1)
        accumulate(b0, None, 0)
        scores(b0 + 2, None, 0)
        accumulate(b0 + 1, None, 1)
        return carry

    lax.fori_loop(0, qi - 1, body, 0)

    @pl.when(qi > 0)
    def _():
        scores(nfull - 1, None, 1)
        accumulate(nfull - 2, None, 0)
        scores(nfull, 0, 0)
        accumulate(nfull - 1, None, 1)

    scores(nfull + 1, 1, 1)
    accumulate(nfull, 0, 0)
    accumulate(nfull + 1, 1, 1)
    for c in range(ngrp):
        o_ref[c * lanes_q:(c + 1) * lanes_q, :] = (acc_scr[c] / l_scr[c]).T.astype(o_ref.dtype)


def _attention(q, k, vt, *, tq=1024, tk=512, lanes_q=256):
    bsz, heads, seq, _ = q.shape
    nq = seq // tq
    ngrp = tq // lanes_q
    blocks = [_nbytes((tq, HEAD_PAD), BF16), _nbytes((seq, HEAD_PAD), BF16), _nbytes((MLA_V, seq), BF16), _nbytes((tq, MLA_V), BF16)]
    scratch = (2 * _nbytes((ngrp, SUBLANES, lanes_q), F32) + _nbytes((ngrp, MLA_V, lanes_q), F32)
               + 2 * _nbytes((ngrp, tk, lanes_q), F32) + 2 * _nbytes((ngrp, SUBLANES, lanes_q), F32))
    return pl.pallas_call(
        functools.partial(_attn_kernel, tq=tq, tk=tk, lanes_q=lanes_q),
        out_shape=jax.ShapeDtypeStruct((bsz * seq, heads * MLA_V), BF16),
        grid=(bsz, heads, nq),
        in_specs=[
            pl.BlockSpec((None, None, tq, HEAD_PAD), lambda b, h, i: (b, h, i, 0)),
            pl.BlockSpec((None, None, seq, HEAD_PAD), lambda b, h, i: (b, h, 0, 0)),
            pl.BlockSpec((None, None, MLA_V, seq), lambda b, h, i: (b, h, 0, 0)),
        ],
        out_specs=pl.BlockSpec((tq, MLA_V), lambda b, h, i: (b * nq + i, h)),
        scratch_shapes=[pltpu.VMEM((ngrp, 1, lanes_q), F32), pltpu.VMEM((ngrp, 1, lanes_q), F32),
                        pltpu.VMEM((ngrp, MLA_V, lanes_q), F32),
                        pltpu.VMEM((ngrp, tk, lanes_q), F32), pltpu.VMEM((ngrp, tk, lanes_q), F32),
                        pltpu.VMEM((ngrp, 1, lanes_q), F32), pltpu.VMEM((ngrp, 1, lanes_q), F32)],
        compiler_params=_params(("arbitrary", "arbitrary", "arbitrary"),
                                _vmem_limit(blocks, scratch, 8 * _nbytes((tk, tq), F32))),
        name="mla_attention",
    )(q, k, vt)


def _causal_conv(t, tail_ref, buf_ref, w_ref, first_of_batch):
    kw = w_ref.shape[0]
    tm = t.shape[0]
    hist = jnp.where(first_of_batch, 0.0, tail_ref[...])
    buf_ref[pl.ds(0, SUBLANES), :] = hist
    buf_ref[pl.ds(SUBLANES, tm), :] = t
    tail_ref[...] = t[tm - SUBLANES:, :]
    acc = t * w_ref[kw - 1:kw, :]
    for k in range(kw - 1):
        acc = acc + buf_ref[pl.ds(SUBLANES - (kw - 1) + k, tm), :] * w_ref[k:k + 1, :]
    return acc


def _conv_in_kernel(h_ref, wb_ref, wc_ref, wu_ref, cw_ref, o_ref, wb_scr, wc_scr, wu_scr, tail_scr, buf_scr, *, tiles_per_batch):
    i = pl.program_id(1)

    @pl.when(i == 0)
    def _():
        wb_scr[...] = wb_ref[...].astype(BF16)
        wc_scr[...] = wc_ref[...].astype(BF16)
        wu_scr[...] = wu_ref[...].astype(BF16)

    h = h_ref[...]
    gc = jnp.dot(h, wc_scr[...], preferred_element_type=F32)
    u = jnp.dot(h, wu_scr[...], preferred_element_type=F32)
    conv = _causal_conv(gc * u, tail_scr, buf_scr, cw_ref, i % tiles_per_batch == 0)
    gb = jnp.dot(h, wb_scr[...], preferred_element_type=F32)
    o_ref[...] = (gb * conv).astype(o_ref.dtype)


def _conv_in(h, w_stack, layer, conv_w, seq, *, tm=512, tn=512):
    t, d = h.shape
    n = w_stack.shape[2] // 3
    nj = n // tn
    tiles_per_batch = seq // tm
    kw = conv_w.shape[0]
    blocks = [_nbytes((tm, d), BF16), 3 * _nbytes((d, tn), F32), _nbytes((tm, tn), BF16)]
    scratch = 3 * _nbytes((d, tn), BF16) + _nbytes((SUBLANES, tn), F32) + _nbytes((tm + SUBLANES, tn), F32)
    wspec = lambda off: pl.BlockSpec((None, d, tn), lambda j, i: (layer, 0, j + off * nj))
    return pl.pallas_call(
        functools.partial(_conv_in_kernel, tiles_per_batch=tiles_per_batch),
        out_shape=jax.ShapeDtypeStruct((t, n), BF16),
        grid=(nj, t // tm),
        in_specs=[
            pl.BlockSpec((tm, d), lambda j, i: (i, 0)),
            wspec(0), wspec(1), wspec(2),
            pl.BlockSpec((kw, tn), lambda j, i: (0, j)),
        ],
        out_specs=pl.BlockSpec((tm, tn), lambda j, i: (i, j)),
        scratch_shapes=[pltpu.VMEM((d, tn), BF16), pltpu.VMEM((d, tn), BF16), pltpu.VMEM((d, tn), BF16),
                        pltpu.VMEM((SUBLANES, tn), F32), pltpu.VMEM((tm + SUBLANES, tn), F32)],
        compiler_params=_params(("arbitrary", "arbitrary"), _vmem_limit(blocks, scratch, 6 * _nbytes((tm, tn), F32))),
        name="conv_in",
    )(h, w_stack, w_stack, w_stack, conv_w)


def _softplus(x):
    return jnp.maximum(x, 0.0) + jnp.log1p(jnp.exp(-jnp.abs(x)))


def _ssd_xbc_kernel(h_ref, w_ref, cw_ref, cb_ref, o_ref, w_scr, tail_scr, buf_scr, *, tiles_per_batch):
    i = pl.program_id(1)

    @pl.when(i == 0)
    def _():
        w_scr[...] = w_ref[...].astype(BF16)

    y = jnp.dot(h_ref[...], w_scr[...], preferred_element_type=F32)
    conv = _causal_conv(y, tail_scr, buf_scr, cw_ref, i % tiles_per_batch == 0) + cb_ref[...]
    o_ref[...] = _silu(conv).astype(o_ref.dtype)


def _ssd_xbc(h, w_stack, layer, conv_w, conv_b, col0, seq, *, tm=512, tn=1024):
    t, d = h.shape
    kw, n = conv_w.shape
    tiles_per_batch = seq // tm
    joff = col0 // tn
    blocks = [_nbytes((tm, d), BF16), _nbytes((d, tn), F32), _nbytes((tm, tn), BF16)]
    scratch = _nbytes((d, tn), BF16) + _nbytes((SUBLANES, tn), F32) + _nbytes((tm + SUBLANES, tn), F32)
    return pl.pallas_call(
        functools.partial(_ssd_xbc_kernel, tiles_per_batch=tiles_per_batch),
        out_shape=jax.ShapeDtypeStruct((t, n), BF16),
        grid=(n // tn, t // tm),
        in_specs=[
            pl.BlockSpec((tm, d), lambda j, i: (i, 0)),
            pl.BlockSpec((None, d, tn), lambda j, i: (layer, 0, j + joff)),
            pl.BlockSpec((kw, tn), lambda j, i: (0, j)),
            pl.BlockSpec((1, tn), lambda j, i: (0, j)),
        ],
        out_specs=pl.BlockSpec((tm, tn), lambda j, i: (i, j)),
        scratch_shapes=[pltpu.VMEM((d, tn), BF16), pltpu.VMEM((SUBLANES, tn), F32), pltpu.VMEM((tm + SUBLANES, tn), F32)],
        compiler_params=_params(("arbitrary", "arbitrary"), _vmem_limit(blocks, scratch, 6 * _nbytes((tm, tn), F32))),
        name="ssd_xbc",
    )(h, w_stack, conv_w, conv_b.reshape(1, -1))


def _ssd_dt_kernel(h_ref, w_ref, b_ref, dt_ref):
    dt_ref[...] = _softplus(jnp.dot(h_ref[...], w_ref[...], preferred_element_type=F32) + b_ref[...])


def _ssd_dt(h, w_dt, dt_bias, *, tm=1024):
    t, d = h.shape
    nh = w_dt.shape[1]
    blocks = [_nbytes((tm, d), BF16), _nbytes((d, LANES), BF16), _nbytes((tm, LANES), F32)]
    return pl.pallas_call(
        _ssd_dt_kernel,
        out_shape=jax.ShapeDtypeStruct((t, nh), F32),
        grid=(t // tm,),
        in_specs=[
            pl.BlockSpec((tm, d), lambda i: (i, 0)),
            pl.BlockSpec((d, nh), lambda i: (0, 0)),
            pl.BlockSpec((1, nh), lambda i: (0, 0)),
        ],
        out_specs=pl.BlockSpec((tm, nh), lambda i: (i, 0)),
        compiler_params=_params(("arbitrary",), _vmem_limit(blocks, temp_bytes=4 * _nbytes((tm, LANES), F32))),
        name="ssd_dt",
    )(h, w_dt, dt_bias.reshape(1, nh))


def _split3(x):
    hi = x.astype(BF16)
    r1 = x - hi.astype(F32)
    mid = r1.astype(BF16)
    lo = (r1 - mid.astype(F32)).astype(BF16)
    return hi, mid, lo


def _dot01(sel, x, *, sel_left):
    out = None
    for part in _split3(x):
        d = (jnp.dot(sel, part, preferred_element_type=F32) if sel_left
             else jnp.dot(part, sel, preferred_element_type=F32))
        out = d if out is None else out + d
    return out


def _ssd_kernel(z_ref, x_ref, b_ref, c_ref, dt_ref, alog_ref, dskip_ref, nw_ref, o_ref, state_scr, *, tt):
    hg = dt_ref.shape[-1]
    gw = x_ref.shape[-1]
    hd = gw // hg
    nch = tt // CHUNK

    @pl.when(pl.program_id(2) == 0)
    def _():
        state_scr[...] = jnp.zeros(state_scr.shape, F32)

    def iota(shape, axis):
        return lax.broadcasted_iota(jnp.int32, shape, axis)

    expand = (iota((hg, gw), 1) // hd == iota((hg, gw), 0)).astype(BF16)
    r2, c2 = iota((tt, tt), 0), iota((tt, tt), 1)
    tri = ((c2 <= r2) & (c2 // CHUNK == r2 // CHUNK)).astype(BF16)
    lrow, lcol = iota((CHUNK, gw), 0), iota((CHUNK, gw), 1) % hd
    half = gw // 2
    bd_keep = iota((half * CHUNK // hd, half), 0) // CHUNK == iota((half * CHUNK // hd, half), 1) // hd

    a_row = -jnp.exp(alog_ref[...])
    dt_all = _dot01(expand, dt_ref[...], sel_left=False)
    dta_all = dt_all * a_row
    acum_all = _dot01(tri, dta_all, sel_left=True)

    for ci in range(nch):
        rows = slice(ci * CHUNK, (ci + 1) * CHUNK)
        x = x_ref[rows, :].astype(F32)
        bm = b_ref[rows, :]
        cm = c_ref[rows, :]
        dt, dta, acum = dt_all[rows], dta_all[rows], acum_all[rows]
        r = jnp.sum(jnp.where(lrow <= lcol, dta, 0.0), axis=0, keepdims=True)
        decay = jnp.exp(jnp.where(lrow >= lcol, acum - r, -jnp.inf))
        cb = lax.dot_general(cm, bm, (((1,), (1,)), ((), ())), preferred_element_type=F32)
        m = (jnp.concatenate([cb] * hg, axis=1) * decay).astype(BF16)
        xdt = x * dt
        xdt16 = xdt.astype(BF16)
        ydiag = []
        for hf in range(2):
            cols = slice(hf * half, (hf + 1) * half)
            blockdiag = jnp.where(bd_keep, jnp.concatenate([xdt16[:, cols]] * (half // hd), axis=0), jnp.zeros((), BF16))
            ydiag.append(jnp.dot(m[:, cols], blockdiag, preferred_element_type=F32))
        last = acum[CHUNK - 1:CHUNK, :]
        xw = (xdt * jnp.exp(last - acum)).astype(BF16)
        st = lax.dot_general(bm, xw, (((0,), (0,)), ((), ())), preferred_element_type=F32)
        prev = state_scr[...]
        yoff = jnp.dot(cm, prev.astype(BF16), preferred_element_type=F32) * jnp.exp(acum)
        state_scr[...] = prev * jnp.exp(last) + st
        y = jnp.concatenate(ydiag, axis=1) + yoff + x * dskip_ref[...]
        gated = y * _silu(z_ref[rows, :].astype(F32))
        o_ref[rows, :] = (_rms(gated) * nw_ref[...]).astype(o_ref.dtype)


def _ssd_scan(z, xbc, dt_g, a_log, d_skip, norm_w, bsz, seq, *, tt=256):
    t, d_inner = z.shape
    groups, _, hg = dt_g.shape
    gw = d_inner // groups
    n = SSM_STATE
    nt = seq // tt
    boff = d_inner // n
    coff = boff + groups
    row = lambda b, g, i: b * nt + i
    blocks = [2 * _nbytes((tt, gw), BF16), 2 * _nbytes((tt, n), BF16), _nbytes((tt, LANES), F32), _nbytes((tt, gw), BF16)]
    return pl.pallas_call(
        functools.partial(_ssd_kernel, tt=tt),
        out_shape=jax.ShapeDtypeStruct((t, d_inner), BF16),
        grid=(bsz, groups, nt),
        in_specs=[
            pl.BlockSpec((tt, gw), lambda b, g, i: (row(b, g, i), g)),
            pl.BlockSpec((tt, gw), lambda b, g, i: (row(b, g, i), g)),
            pl.BlockSpec((tt, n), lambda b, g, i: (row(b, g, i), boff + g)),
            pl.BlockSpec((tt, n), lambda b, g, i: (row(b, g, i), coff + g)),
            pl.BlockSpec((None, tt, hg), lambda b, g, i: (g, row(b, g, i), 0)),
            pl.BlockSpec((None, 1, gw), lambda b, g, i: (g, 0, 0)),
            pl.BlockSpec((None, 1, gw), lambda b, g, i: (g, 0, 0)),
            pl.BlockSpec((1, gw), lambda b, g, i: (0, g)),
        ],
        out_specs=pl.BlockSpec((tt, gw), lambda b, g, i: (row(b, g, i), g)),
        scratch_shapes=[pltpu.VMEM((n, gw), F32)],
        compiler_params=_params(("arbitrary", "arbitrary", "arbitrary"),
                                _vmem_limit(blocks, _nbytes((n, gw), F32), 24 * _nbytes((tt, gw), F32))),
        name="ssd_scan",
    )(z, xbc, xbc, xbc, dt_g, a_log, d_skip, norm_w.reshape(1, d_inner))


def _pad_heads(w, heads, width, padded):
    r = w.shape[0]
    w = w.reshape(r, heads, width)
    return jnp.pad(w, ((0, 0), (0, 0), (0, padded - width))).reshape(r, heads * padded)


def kernel(x, c, positions, ada_w, ada_b, norm_pre, norm_post, mla_w_down, mla_q_norm, mla_w_uq, mla_kv_norm, mla_w_ukv, mla_w_o, conv_w_in, conv_w, conv_w_out, ssm_w_in, ssm_conv_w, ssm_conv_b, ssm_dt_bias, ssm_a_log, ssm_d, ssm_norm, ssm_w_out, mlp_up, mlp_down):
    bsz, seq, d = x.shape
    depth = ada_w.shape[0]
    t = bsz * seq
    xf = x.reshape(t, d)

    mods = _ada_all(c, ada_w, ada_b).reshape(depth, 2, bsz, 3, d)
    cos_t, sin_t = _rope_tables(positions)
    mlp_down16 = mlp_down.astype(BF16)

    h = _norm_mod(xf, mods[0, 0], norm_pre[0, 0], seq)
    for i in range(depth):
        kind, j = i % N_MIXERS, i // N_MIXERS
        if kind == 0:
            w_uq = _pad_heads(mla_w_uq[j], MLA_HEADS, MLA_NOPE + MLA_ROPE, HEAD_PAD).astype(BF16)
            cq, ckv, kr = _mla_down(h, mla_w_down, j, mla_q_norm[j], mla_kv_norm[j], cos_t, sin_t)
            q = _q_up(cq, w_uq, cos_t, sin_t, bsz, seq)
            k, vt = _kv_up(ckv, mla_w_ukv[j].astype(BF16), kr, bsz, seq)
            a = _attention(q, k, vt)
            w_o = mla_w_o
        elif kind == 1:
            a = _conv_in(h, conv_w_in, j, conv_w[j], seq)
            w_o = conv_w_out
        else:
            d_inner = ssm_w_out.shape[1]
            nh = ssm_dt_bias.shape[1]
            n_main = ssm_w_in.shape[2] - nh
            z = _mm_w(h, ssm_w_in, j, epilogue=_identity, ncols=d_inner)
            xbc = _ssd_xbc(h, ssm_w_in, j, ssm_conv_w[j], ssm_conv_b[j], d_inner, seq)
            dt = _ssd_dt(h, ssm_w_in[j, :, n_main:].astype(BF16), ssm_dt_bias[j])
            hg = nh // SSM_GROUPS
            gw = d_inner // SSM_GROUPS
            dt_g = dt.reshape(t, SSM_GROUPS, hg).transpose(1, 0, 2)
            per_lane = lambda p: jnp.repeat(p.reshape(SSM_GROUPS, hg), SSM_HEAD_DIM, axis=1).reshape(SSM_GROUPS, 1, gw)
            a = _ssd_scan(z, xbc, dt_g, per_lane(ssm_a_log[j]), per_lane(ssm_d[j]), ssm_norm[j], bsz, seq)
            w_o = ssm_w_out
        xf, h = _mm_resid(a, w_o, j, xf, mods[i, 0], norm_post[i, 0], (mods[i, 1], norm_pre[i, 1]), seq,
                          resident=True, tm=512)
        u = _mm_w(h, mlp_up, i, epilogue=_relu2)
        nxt = (mods[i + 1, 0], norm_pre[i + 1, 0]) if i + 1 < depth else None
        xf, h = _mm_resid(u, mlp_down16, i, xf, mods[i, 1], norm_post[i, 1], nxt, seq, resident=False, tm=1024)
    return xf.reshape(bsz, seq, d)
```

```python
import functools
import math

import jax
import jax.numpy as jnp
from jax import lax
from jax.experimental import pallas as pl
from jax.experimental.pallas import tpu as pltpu

F32 = jnp.float32
BF16 = jnp.bfloat16

EPS = 1e-6
CHUNK = 64
N_MIXERS = 3
MLA_HEADS = 16
MLA_LORA = 512
MLA_NOPE = 128
MLA_ROPE = 64
MLA_V = 128
ROPE_THETA = 10000.0
SSM_HEAD_DIM = 64
SSM_GROUPS = 8
SSM_STATE = 128

LANES = 128
SUBLANES = 8
V7X_VMEM_BYTES = 64 * 1024 * 1024
HEAD_PAD = 2 * LANES


def _vmem_limit(block_bytes, scratch_bytes=0, temp_bytes=0):
    need = 2 * sum(block_bytes) + scratch_bytes + temp_bytes + (4 << 20)
    return int(min(need, V7X_VMEM_BYTES - (6 << 20)))


def _nbytes(shape, dtype):
    return math.prod(shape) * jnp.dtype(dtype).itemsize


def _params(sem, limit):
    return pltpu.CompilerParams(dimension_semantics=sem, vmem_limit_bytes=limit)


def _rms(x):
    return x * lax.rsqrt(jnp.mean(x * x, axis=-1, keepdims=True) + EPS)


def _modulated_norm(x, mod_ref, g_ref):
    h = _rms(x) * g_ref[...]
    return h * (1.0 + mod_ref[1:2, :]) + mod_ref[0:1, :]


def _silu(x):
    return x * (1.0 / (1.0 + jnp.exp(-x)))


def _ada_kernel(c_ref, w_ref, b_ref, o_ref):
    c_act = _silu(c_ref[...]).astype(BF16)
    y = jnp.dot(c_act, w_ref[...].astype(BF16), preferred_element_type=F32)
    o_ref[...] = y + b_ref[...]


def _ada_all(c, ada_w, ada_b):
    nsub = ada_w.shape[0] * ada_w.shape[1]
    bsz, d = c.shape
    n = ada_w.shape[-1]
    w = ada_w.reshape(nsub, d, n)
    b = ada_b.reshape(nsub, 1, n)
    tn = 1024
    blocks = [_nbytes((d, tn), F32), _nbytes((bsz, tn), F32)]
    return pl.pallas_call(
        _ada_kernel,
        out_shape=jax.ShapeDtypeStruct((nsub, bsz, n), F32),
        grid=(nsub, n // tn),
        in_specs=[
            pl.BlockSpec((bsz, d), lambda s, j: (0, 0)),
            pl.BlockSpec((None, d, tn), lambda s, j: (s, 0, j)),
            pl.BlockSpec((None, 1, tn), lambda s, j: (s, 0, j)),
        ],
        out_specs=pl.BlockSpec((None, bsz, tn), lambda s, j: (s, 0, j)),
        compiler_params=_params(("arbitrary", "arbitrary"), _vmem_limit(blocks, temp_bytes=_nbytes((d, tn), BF16))),
        name="ada_mod",
    )(c, w, b)


def _rope_kernel(pos_ref, freq_ref, cos_ref, sin_ref):
    ang = pos_ref[...] * freq_ref[...]
    lane = lax.broadcasted_iota(jnp.int32, ang.shape, 1)
    half = MLA_ROPE // 2
    cos_ref[...] = jnp.where(lane < MLA_ROPE, jnp.cos(ang), 0.0)
    s = jnp.sin(ang)
    sin_ref[...] = jnp.where(lane < half, -s, jnp.where(lane < MLA_ROPE, s, 0.0))


def _rope_tables(positions):
    t = positions.size
    half = MLA_ROPE // 2
    inv_freq = ROPE_THETA ** (-jnp.arange(0, MLA_ROPE, 2, dtype=F32) / MLA_ROPE)
    freq = jnp.concatenate([inv_freq, inv_freq, jnp.zeros((LANES - 2 * half,), F32)]).reshape(1, LANES)
    pos = positions.astype(F32).reshape(t, 1)
    tm = min(t, 1024)
    out = jax.ShapeDtypeStruct((t, LANES), F32)
    return pl.pallas_call(
        _rope_kernel,
        out_shape=(out, out),
        grid=(t // tm,),
        in_specs=[pl.BlockSpec((tm, 1), lambda i: (i, 0)), pl.BlockSpec((1, LANES), lambda i: (0, 0))],
        out_specs=(pl.BlockSpec((tm, LANES), lambda i: (i, 0)), pl.BlockSpec((tm, LANES), lambda i: (i, 0))),
        compiler_params=_params(("arbitrary",), _vmem_limit([_nbytes((tm, LANES), F32)] * 3, temp_bytes=8 << 20)),
        name="rope_tables",
    )(pos, freq)


def _rope_slab(z, cos_ref, sin_ref):
    half = MLA_ROPE // 2
    lane = lax.broadcasted_iota(jnp.int32, z.shape, 1)
    partner = jnp.where(lane < half, pltpu.roll(z, LANES - half, axis=1), pltpu.roll(z, half, axis=1))
    return z * cos_ref[...] + partner * sin_ref[...]


def _relu2(y):
    r = jnp.maximum(y, 0.0)
    return r * r


def _norm_mod_kernel(x_ref, mod_ref, g_ref, h_ref):
    h_ref[...] = _modulated_norm(x_ref[...], mod_ref, g_ref).astype(BF16)


def _norm_mod(x, mod, g, seq, *, tm=1024):
    t, d = x.shape
    tiles_per_batch = seq // tm
    blocks = [_nbytes((tm, d), F32), _nbytes((tm, d), BF16)]
    return pl.pallas_call(
        _norm_mod_kernel,
        out_shape=jax.ShapeDtypeStruct((t, d), BF16),
        grid=(t // tm,),
        in_specs=[
            pl.BlockSpec((tm, d), lambda i: (i, 0)),
            pl.BlockSpec((None, 3, d), lambda i: (i // tiles_per_batch, 0, 0)),
            pl.BlockSpec((1, d), lambda i: (0, 0)),
        ],
        out_specs=pl.BlockSpec((tm, d), lambda i: (i, 0)),
        compiler_params=_params(("arbitrary",), _vmem_limit(blocks, temp_bytes=2 * _nbytes((tm, d), F32))),
        name="norm_mod",
    )(x, mod, g.reshape(1, d))


def _mm_w_kernel(h_ref, w_ref, o_ref, w_scr, *, epilogue):
    @pl.when(pl.program_id(1) == 0)
    def _():
        w_scr[...] = w_ref[...].astype(BF16)

    y = jnp.dot(h_ref[...], w_scr[...], preferred_element_type=F32)
    o_ref[...] = epilogue(y).astype(o_ref.dtype)


def _identity(y):
    return y


def _mm_w(h, w_stack, layer, *, epilogue, ncols=None, tm=1024, tn=1024):
    t, d = h.shape
    n = w_stack.shape[2] if ncols is None else ncols
    blocks = [_nbytes((tm, d), BF16), _nbytes((d, tn), F32), _nbytes((tm, tn), BF16)]
    return pl.pallas_call(
        functools.partial(_mm_w_kernel, epilogue=epilogue),
        out_shape=jax.ShapeDtypeStruct((t, n), BF16),
        grid=(n // tn, t // tm),
        in_specs=[
            pl.BlockSpec((tm, d), lambda j, i: (i, 0)),
            pl.BlockSpec((None, d, tn), lambda j, i: (layer, 0, j)),
        ],
        out_specs=pl.BlockSpec((tm, tn), lambda j, i: (i, j)),
        scratch_shapes=[pltpu.VMEM((d, tn), BF16)],
        compiler_params=_params(("arbitrary", "arbitrary"),
                                _vmem_limit(blocks, _nbytes((d, tn), BF16), 2 * _nbytes((tm, tn), F32))),
        name="mm_w",
    )(h, w_stack)


_EPILOGUE_ROWS = 256


def _mm_resid_kernel(*refs, nk, emit_next):
    a_ref, w_ref, x_ref, mod_ref, g_ref = refs[:5]
    refs = refs[5:]
    if emit_next:
        modn_ref, gn_ref, o_ref, hn_ref = refs[:4]
        refs = refs[4:]
    else:
        o_ref = refs[0]
        refs = refs[1:]
    i, k = pl.program_id(0), pl.program_id(1)
    tm = o_ref.shape[0]

    if nk > 1:
        x_scr, x_sem = refs

        def x_copy():
            return pltpu.make_async_copy(x_ref.at[pl.ds(pl.multiple_of(i * tm, tm), tm), :], x_scr, x_sem)

        @pl.when(k == 0)
        def _():
            x_copy().start()
            o_ref[...] = jnp.dot(a_ref[...], w_ref[...], preferred_element_type=F32)

        if nk > 2:
            @pl.when((k > 0) & (k < nk - 1))
            def _():
                o_ref[...] += jnp.dot(a_ref[...], w_ref[...], preferred_element_type=F32)
    else:
        x_scr = x_ref

    def finish():
        gate_g = mod_ref[2:3, :] * g_ref[...]
        if emit_next:
            scale_g = gn_ref[...] * (1.0 + modn_ref[1:2, :])
        for r0 in range(0, tm, _EPILOGUE_ROWS):
            rows = slice(r0, r0 + _EPILOGUE_ROWS)
            y = jnp.dot(a_ref[rows, :], w_ref[...], preferred_element_type=F32)
            if nk > 1:
                y = o_ref[rows, :] + y
            x_new = x_scr[rows, :] + _rms(y) * gate_g
            o_ref[rows, :] = x_new
            if emit_next:
                hn_ref[rows, :] = (_rms(x_new) * scale_g + modn_ref[0:1, :]).astype(BF16)

    if nk > 1:
        @pl.when(k == nk - 1)
        def _():
            x_copy().wait()
            finish()
    else:
        finish()


def _mm_resid(a, w_stack, layer, x, mod, g, nxt, seq, *, tm=512, tk_max=2048):
    t, kdim = a.shape
    d = w_stack.shape[2]
    tk = min(kdim, tk_max)
    nk = kdim // tk
    tiles_per_batch = seq // tm
    emit_next = nxt is not None
    batch_map = lambda i, k: (i // tiles_per_batch, 0, 0)
    fixed = lambda i, k: (0, 0)
    x_spec = pl.BlockSpec((tm, d), lambda i, k: (i, 0)) if nk == 1 else pl.BlockSpec(memory_space=pl.ANY)
    in_specs = [
        pl.BlockSpec((tm, tk), lambda i, k: (i, k)),
        pl.BlockSpec((None, tk, d), lambda i, k: (layer, k, 0)),
        x_spec,
        pl.BlockSpec((None, 3, d), batch_map),
        pl.BlockSpec((1, d), fixed),
    ]
    args = [a, w_stack, x, mod, g.reshape(1, d)]
    out_shape = [jax.ShapeDtypeStruct((t, d), F32)]
    out_specs = [pl.BlockSpec((tm, d), lambda i, k: (i, 0))]
    blocks = [_nbytes((tm, tk), BF16), _nbytes((tk, d), BF16), _nbytes((tm, d), F32)]
    if emit_next:
        in_specs += [pl.BlockSpec((None, 3, d), batch_map), pl.BlockSpec((1, d), fixed)]
        args += [nxt[0], nxt[1].reshape(1, d)]
        out_shape.append(jax.ShapeDtypeStruct((t, d), BF16))
        out_specs.append(pl.BlockSpec((tm, d), lambda i, k: (i, 0)))
        blocks.append(_nbytes((tm, d), BF16))
    if nk == 1:
        scratch, scratch_bytes = [], 0
        blocks.append(_nbytes((tm, d), F32))
    else:
        scratch, scratch_bytes = [pltpu.VMEM((tm, d), F32), pltpu.SemaphoreType.DMA(())], _nbytes((tm, d), F32)
    outs = pl.pallas_call(
        functools.partial(_mm_resid_kernel, nk=nk, emit_next=emit_next),
        out_shape=tuple(out_shape),
        grid=(t // tm, nk),
        in_specs=in_specs,
        out_specs=tuple(out_specs),
        scratch_shapes=scratch,
        compiler_params=_params(("arbitrary", "arbitrary"),
                                _vmem_limit(blocks, scratch_bytes, _nbytes((tm, d), F32) + 8 * _nbytes((_EPILOGUE_ROWS, d), F32))),
        name="mm_resid",
    )(*args)
    return (outs[0], outs[1]) if emit_next else (outs[0], None)


def _mla_down_kernel(h_ref, w_ref, qn_ref, kvn_ref, cos_ref, sin_ref, cq_ref, ckv_ref, kr_ref, w_scr):
    @pl.when(pl.program_id(0) == 0)
    def _():
        w_scr[...] = w_ref[...].astype(BF16)

    y = jnp.dot(h_ref[...], w_scr[...], preferred_element_type=F32)
    cq_ref[...] = (_rms(y[:, :MLA_LORA]) * qn_ref[...]).astype(BF16)
    ckv_ref[...] = (_rms(y[:, MLA_LORA:2 * MLA_LORA]) * kvn_ref[...]).astype(BF16)
    slab = jnp.concatenate([y[:, 2 * MLA_LORA:], jnp.zeros((y.shape[0], LANES - MLA_ROPE), F32)], axis=1)
    kr_ref[...] = _rope_slab(slab, cos_ref, sin_ref).astype(BF16)


def _mla_down(h, w_stack, layer, q_norm, kv_norm, cos_t, sin_t, *, tm=512):
    t, d = h.shape
    n = w_stack.shape[2]
    blocks = [_nbytes((tm, d), BF16), _nbytes((d, n), F32), 3 * _nbytes((tm, MLA_LORA), BF16), 2 * _nbytes((tm, LANES), F32)]
    row = lambda i: (i, 0)
    fixed = lambda i: (0, 0)
    return pl.pallas_call(
        _mla_down_kernel,
        out_shape=(jax.ShapeDtypeStruct((t, MLA_LORA), BF16), jax.ShapeDtypeStruct((t, MLA_LORA), BF16),
                   jax.ShapeDtypeStruct((t, LANES), BF16)),
        grid=(t // tm,),
        in_specs=[
            pl.BlockSpec((tm, d), row),
            pl.BlockSpec((None, d, n), lambda i: (layer, 0, 0)),
            pl.BlockSpec((1, MLA_LORA), fixed),
            pl.BlockSpec((1, MLA_LORA), fixed),
            pl.BlockSpec((tm, LANES), row),
            pl.BlockSpec((tm, LANES), row),
        ],
        out_specs=(pl.BlockSpec((tm, MLA_LORA), row), pl.BlockSpec((tm, MLA_LORA), row), pl.BlockSpec((tm, LANES), row)),
        scratch_shapes=[pltpu.VMEM((d, n), BF16)],
        compiler_params=_params(("arbitrary",), _vmem_limit(blocks, _nbytes((d, n), BF16), 3 * _nbytes((tm, n), F32))),
        name="mla_down",
    )(h, w_stack, q_norm.reshape(1, -1), kv_norm.reshape(1, -1), cos_t, sin_t)


def _q_up_kernel(a_ref, w_ref, cos_ref, sin_ref, q_ref, *, heads, scale):
    y = jnp.dot(a_ref[...], w_ref[...], preferred_element_type=F32) * scale
    for h in range(heads):
        base = h * HEAD_PAD
        q_ref[h, :, :LANES] = y[:, base:base + LANES].astype(BF16)
        q_ref[h, :, LANES:] = _rope_slab(y[:, base + LANES:base + HEAD_PAD], cos_ref, sin_ref).astype(BF16)


def _kv_up_kernel(a_ref, w_ref, kr_ref, k_ref, vt_ref, *, heads):
    y = jnp.dot(a_ref[...], w_ref[...], preferred_element_type=F32)
    kr = kr_ref[...]
    for h in range(heads):
        base = h * (MLA_NOPE + MLA_V)
        k_ref[h, :, :LANES] = y[:, base:base + MLA_NOPE].astype(BF16)
        k_ref[h, :, LANES:] = kr
        vt_ref[h] = y[:, base + MLA_NOPE:base + MLA_NOPE + MLA_V].T.astype(BF16)


def _q_up(cq, w_uq, cos_t, sin_t, bsz, seq, *, tm=1024, heads=4):
    t, r = cq.shape
    tiles_per_batch = seq // tm
    scale = math.log2(math.e) / math.sqrt(MLA_NOPE + MLA_ROPE)
    tn = heads * HEAD_PAD
    blocks = [_nbytes((tm, r), BF16), _nbytes((r, tn), BF16), 2 * _nbytes((tm, LANES), F32), _nbytes((heads, tm, HEAD_PAD), BF16)]
    return pl.pallas_call(
        functools.partial(_q_up_kernel, heads=heads, scale=scale),
        out_shape=jax.ShapeDtypeStruct((bsz, MLA_HEADS, seq, HEAD_PAD), BF16),
        grid=(t // tm, MLA_HEADS // heads),
        in_specs=[
            pl.BlockSpec((tm, r), lambda i, j: (i, 0)),
            pl.BlockSpec((r, tn), lambda i, j: (0, j)),
            pl.BlockSpec((tm, LANES), lambda i, j: (i, 0)),
            pl.BlockSpec((tm, LANES), lambda i, j: (i, 0)),
        ],
        out_specs=pl.BlockSpec((None, heads, tm, HEAD_PAD),
                               lambda i, j: (i // tiles_per_batch, j, i % tiles_per_batch, 0)),
        compiler_params=_params(("arbitrary", "arbitrary"), _vmem_limit(blocks, temp_bytes=2 * _nbytes((tm, tn), F32))),
        name="mla_q_up",
    )(cq, w_uq, cos_t, sin_t)


def _kv_up(ckv, w_ukv, kr, bsz, seq, *, tm=1024, heads=4):
    t, r = ckv.shape
    tiles_per_batch = seq // tm
    tn = heads * (MLA_NOPE + MLA_V)
    blocks = [_nbytes((tm, r), BF16), _nbytes((r, tn), BF16), _nbytes((tm, LANES), BF16),
              _nbytes((heads, tm, HEAD_PAD), BF16), _nbytes((heads, tm, MLA_V), BF16)]
    out_map = lambda i, j: (i // tiles_per_batch, j, i % tiles_per_batch, 0)
    vt_map = lambda i, j: (i // tiles_per_batch, j, 0, i % tiles_per_batch)
    return pl.pallas_call(
        functools.partial(_kv_up_kernel, heads=heads),
        out_shape=(jax.ShapeDtypeStruct((bsz, MLA_HEADS, seq, HEAD_PAD), BF16),
                   jax.ShapeDtypeStruct((bsz, MLA_HEADS, MLA_V, seq), BF16)),
        grid=(t // tm, MLA_HEADS // heads),
        in_specs=[
            pl.BlockSpec((tm, r), lambda i, j: (i, 0)),
            pl.BlockSpec((r, tn), lambda i, j: (0, j)),
            pl.BlockSpec((tm, LANES), lambda i, j: (i, 0)),
        ],
        out_specs=(pl.BlockSpec((None, heads, tm, HEAD_PAD), out_map), pl.BlockSpec((None, heads, MLA_V, tm), vt_map)),
        compiler_params=_params(("arbitrary", "arbitrary"), _vmem_limit(blocks, temp_bytes=2 * _nbytes((tm, tn), F32))),
        name="mla_kv_up",
    )(ckv, w_ukv, kr)


def _attn_kernel(q_ref, k_ref, vt_ref, o_ref, m_scr, l_scr, acc_scr, sa_scr, sb_scr, bma_scr, bmb_scr, *, tq, tk, lanes_q):
    qi = pl.program_id(2)
    ngrp = tq // lanes_q
    ndiag = tq // tk
    assert ndiag == 2, "the two-slot score ring below assumes two key blocks per query tile"
    nfull = qi * ndiag
    ring = ((sa_scr, bma_scr), (sb_scr, bmb_scr))
    m_scr[...] = jnp.full(m_scr.shape, -jnp.inf, F32)
    l_scr[...] = jnp.zeros(l_scr.shape, F32)
    acc_scr[...] = jnp.zeros(acc_scr.shape, F32)

    def skipped(c, diag):
        return diag is not None and (c + 1) * lanes_q <= diag * tk

    def scores(blk, diag, slot):
        s_scr, bm_scr = ring[slot]
        kb = k_ref[pl.ds(pl.multiple_of(blk * tk, tk), tk), :]
        for c in range(ngrp):
            if skipped(c, diag):
                continue
            qc = q_ref[c * lanes_q:(c + 1) * lanes_q, :]
            s = lax.dot_general(kb, qc, (((1,), (1,)), ((), ())), preferred_element_type=F32)
            if diag is not None and c * lanes_q < (diag + 1) * tk:
                k_chunk = (lax.broadcasted_iota(jnp.int32, s.shape, 0) + diag * tk) // CHUNK
                q_chunk = (lax.broadcasted_iota(jnp.int32, s.shape, 1) + c * lanes_q) // CHUNK
                s = jnp.where(k_chunk <= q_chunk, s, -jnp.inf)
            s_scr[c] = s
            bm_scr[c] = jnp.max(s, axis=0, keepdims=True)

    def accumulate(blk, diag, slot):
        s_scr, bm_scr = ring[slot]
        vtb = vt_ref[:, pl.ds(pl.multiple_of(blk * tk, tk), tk)]
        for c in range(ngrp):
            if skipped(c, diag):
                continue
            m_prev = m_scr[c]
            m_new = jnp.maximum(m_prev, bm_scr[c])
            alpha = jnp.exp2(m_prev - m_new)
            p = jnp.exp2(s_scr[c] - m_new)
            l_scr[c] = alpha * l_scr[c] + jnp.sum(p, axis=0, keepdims=True)
            acc_scr[c] = alpha * acc_scr[c] + jnp.dot(vtb, p.astype(BF16), preferred_element_type=F32)
            m_scr[c] = m_new

    @pl.when(qi > 0)
    def _():
        scores(0, None, 0)

    @pl.when(qi == 0)
    def _():
        scores(0, 0, 0)

    def body(i, carry):
        b0 = 2 * i
        scores(b0 + 1, None, 1)
        accumulate(b0, None, 0)
        scores(b0 + 2, None, 0)
        accumulate(b0 + 1, None, 1)
        return carry

    lax.fori_loop(0, qi - 1, body, 0)

    @pl.when(qi > 0)
    def _():
        scores(nfull - 1, None, 1)
        accumulate(nfull - 2, None, 0)
        scores(nfull, 0, 0)
        accumulate(nfull - 1, None, 1)

    scores(nfull + 1, 1, 1)
    accumulate(nfull, 0, 0)
    accumulate(nfull + 1, 1, 1)
    for c in range(ngrp):
        o_ref[c * lanes_q:(c + 1) * lanes_q, :] = (acc_scr[c] / l_scr[c]).T.astype(o_ref.dtype)


def _attention(q, k, vt, *, tq=1024, tk=512, lanes_q=256):
    bsz, heads, seq, _ = q.shape
    nq = seq // tq
    ngrp = tq // lanes_q
    blocks = [_nbytes((tq, HEAD_PAD), BF16), _nbytes((seq, HEAD_PAD), BF16), _nbytes((MLA_V, seq), BF16), _nbytes((tq, MLA_V), BF16)]
    scratch = (2 * _nbytes((ngrp, SUBLANES, lanes_q), F32) + _nbytes((ngrp, MLA_V, lanes_q), F32)
               + 2 * _nbytes((ngrp, tk, lanes_q), F32) + 2 * _nbytes((ngrp, SUBLANES, lanes_q), F32))
    return pl.pallas_call(
        functools.partial(_attn_kernel, tq=tq, tk=tk, lanes_q=lanes_q),
        out_shape=jax.ShapeDtypeStruct((bsz * seq, heads * MLA_V), BF16),
        grid=(bsz, heads, nq),
        in_specs=[
            pl.BlockSpec((None, None, tq, HEAD_PAD), lambda b, h, i: (b, h, i, 0)),
            pl.BlockSpec((None, None, seq, HEAD_PAD), lambda b, h, i: (b, h, 0, 0)),
            pl.BlockSpec((None, None, MLA_V, seq), lambda b, h, i: (b, h, 0, 0)),
        ],
        out_specs=pl.BlockSpec((tq, MLA_V), lambda b, h, i: (b * nq + i, h)),
        scratch_shapes=[pltpu.VMEM((ngrp, 1, lanes_q), F32), pltpu.VMEM((ngrp, 1, lanes_q), F32),
                        pltpu.VMEM((ngrp, MLA_V, lanes_q), F32),
                        pltpu.VMEM((ngrp, tk, lanes_q), F32), pltpu.VMEM((ngrp, tk, lanes_q), F32),
                        pltpu.VMEM((ngrp, 1, lanes_q), F32), pltpu.VMEM((ngrp, 1, lanes_q), F32)],
        compiler_params=_params(("arbitrary", "arbitrary", "arbitrary"),
                                _vmem_limit(blocks, scratch, 8 * _nbytes((tk, tq), F32))),
        name="mla_attention",
    )(q, k, vt)


def _causal_conv(t, tail_ref, buf_ref, w_ref, first_of_batch):
    kw = w_ref.shape[0]
    tm = t.shape[0]
    hist = jnp.where(first_of_batch, 0.0, tail_ref[...])
    buf_ref[pl.ds(0, SUBLANES), :] = hist
    buf_ref[pl.ds(SUBLANES, tm), :] = t
    tail_ref[...] = t[tm - SUBLANES:, :]
    acc = t * w_ref[kw - 1:kw, :]
    for k in range(kw - 1):
        acc = acc + buf_ref[pl.ds(SUBLANES - (kw - 1) + k, tm), :] * w_ref[k:k + 1, :]
    return acc


def _conv_in_kernel(h_ref, wb_ref, wc_ref, wu_ref, cw_ref, o_ref, wb_scr, wc_scr, wu_scr, tail_scr, buf_scr, *, tiles_per_batch):
    i = pl.program_id(1)

    @pl.when(i == 0)
    def _():
        wb_scr[...] = wb_ref[...].astype(BF16)
        wc_scr[...] = wc_ref[...].astype(BF16)
        wu_scr[...] = wu_ref[...].astype(BF16)

    h = h_ref[...]
    gc = jnp.dot(h, wc_scr[...], preferred_element_type=F32)
    u = jnp.dot(h, wu_scr[...], preferred_element_type=F32)
    conv = _causal_conv(gc * u, tail_scr, buf_scr, cw_ref, i % tiles_per_batch == 0)
    gb = jnp.dot(h, wb_scr[...], preferred_element_type=F32)
    o_ref[...] = (gb * conv).astype(o_ref.dtype)


def _conv_in(h, w_stack, layer, conv_w, seq, *, tm=512, tn=512):
    t, d = h.shape
    n = w_stack.shape[2] // 3
    nj = n // tn
    tiles_per_batch = seq // tm
    kw = conv_w.shape[0]
    blocks = [_nbytes((tm, d), BF16), 3 * _nbytes((d, tn), F32), _nbytes((tm, tn), BF16)]
    scratch = 3 * _nbytes((d, tn), BF16) + _nbytes((SUBLANES, tn), F32) + _nbytes((tm + SUBLANES, tn), F32)
    wspec = lambda off: pl.BlockSpec((None, d, tn), lambda j, i: (layer, 0, j + off * nj))
    return pl.pallas_call(
        functools.partial(_conv_in_kernel, tiles_per_batch=tiles_per_batch),
        out_shape=jax.ShapeDtypeStruct((t, n), BF16),
        grid=(nj, t // tm),
        in_specs=[
            pl.BlockSpec((tm, d), lambda j, i: (i, 0)),
            wspec(0), wspec(1), wspec(2),
            pl.BlockSpec((kw, tn), lambda j, i: (0, j)),
        ],
        out_specs=pl.BlockSpec((tm, tn), lambda j, i: (i, j)),
        scratch_shapes=[pltpu.VMEM((d, tn), BF16), pltpu.VMEM((d, tn), BF16), pltpu.VMEM((d, tn), BF16),
                        pltpu.VMEM((SUBLANES, tn), F32), pltpu.VMEM((tm + SUBLANES, tn), F32)],
        compiler_params=_params(("arbitrary", "arbitrary"), _vmem_limit(blocks, scratch, 6 * _nbytes((tm, tn), F32))),
        name="conv_in",
    )(h, w_stack, w_stack, w_stack, conv_w)


def _softplus(x):
    return jnp.maximum(x, 0.0) + jnp.log1p(jnp.exp(-jnp.abs(x)))


def _ssd_xbc_kernel(h_ref, w_ref, cw_ref, cb_ref, o_ref, w_scr, tail_scr, buf_scr, *, tiles_per_batch):
    i = pl.program_id(1)

    @pl.when(i == 0)
    def _():
        w_scr[...] = w_ref[...].astype(BF16)

    y = jnp.dot(h_ref[...], w_scr[...], preferred_element_type=F32)
    conv = _causal_conv(y, tail_scr, buf_scr, cw_ref, i % tiles_per_batch == 0) + cb_ref[...]
    o_ref[...] = _silu(conv).astype(o_ref.dtype)


def _ssd_xbc(h, w_stack, layer, conv_w, conv_b, col0, seq, *, tm=512, tn=1024):
    t, d = h.shape
    kw, n = conv_w.shape
    tiles_per_batch = seq // tm
    joff = col0 // tn
    blocks = [_nbytes((tm, d), BF16), _nbytes((d, tn), F32), _nbytes((tm, tn), BF16)]
    scratch = _nbytes((d, tn), BF16) + _nbytes((SUBLANES, tn), F32) + _nbytes((tm + SUBLANES, tn), F32)
    return pl.pallas_call(
        functools.partial(_ssd_xbc_kernel, tiles_per_batch=tiles_per_batch),
        out_shape=jax.ShapeDtypeStruct((t, n), BF16),
        grid=(n // tn, t // tm),
        in_specs=[
            pl.BlockSpec((tm, d), lambda j, i: (i, 0)),
            pl.BlockSpec((None, d, tn), lambda j, i: (layer, 0, j + joff)),
            pl.BlockSpec((kw, tn), lambda j, i: (0, j)),
            pl.BlockSpec((1, tn), lambda j, i: (0, j)),
        ],
        out_specs=pl.BlockSpec((tm, tn), lambda j, i: (i, j)),
        scratch_shapes=[pltpu.VMEM((d, tn), BF16), pltpu.VMEM((SUBLANES, tn), F32), pltpu.VMEM((tm + SUBLANES, tn), F32)],
        compiler_params=_params(("arbitrary", "arbitrary"), _vmem_limit(blocks, scratch, 6 * _nbytes((tm, tn), F32))),
        name="ssd_xbc",
    )(h, w_stack, conv_w, conv_b.reshape(1, -1))


def _ssd_dt_kernel(h_ref, w_ref, b_ref, dt_ref):
    dt_ref[...] = _softplus(jnp.dot(h_ref[...], w_ref[...], preferred_element_type=F32) + b_ref[...])


def _ssd_dt(h, w_dt, dt_bias, *, tm=1024):
    t, d = h.shape
    nh = w_dt.shape[1]
    blocks = [_nbytes((tm, d), BF16), _nbytes((d, LANES), BF16), _nbytes((tm, LANES), F32)]
    return pl.pallas_call(
        _ssd_dt_kernel,
        out_shape=jax.ShapeDtypeStruct((t, nh), F32),
        grid=(t // tm,),
        in_specs=[
            pl.BlockSpec((tm, d), lambda i: (i, 0)),
            pl.BlockSpec((d, nh), lambda i: (0, 0)),
            pl.BlockSpec((1, nh), lambda i: (0, 0)),
        ],
        out_specs=pl.BlockSpec((tm, nh), lambda i: (i, 0)),
        compiler_params=_params(("arbitrary",), _vmem_limit(blocks, temp_bytes=4 * _nbytes((tm, LANES), F32))),
        name="ssd_dt",
    )(h, w_dt, dt_bias.reshape(1, nh))


def _split3(x):
    hi = x.astype(BF16)
    r1 = x - hi.astype(F32)
    mid = r1.astype(BF16)
    lo = (r1 - mid.astype(F32)).astype(BF16)
    return hi, mid, lo


def _dot01(sel, x, *, sel_left):
    out = None
    for part in _split3(x):
        d = (jnp.dot(sel, part, preferred_element_type=F32) if sel_left
             else jnp.dot(part, sel, preferred_element_type=F32))
        out = d if out is None else out + d
    return out


def _ssd_kernel(z_ref, x_ref, b_ref, c_ref, dt_ref, alog_ref, dskip_ref, nw_ref, o_ref, state_scr, *, tt):
    hg = dt_ref.shape[-1]
    gw = x_ref.shape[-1]
    hd = gw // hg
    nch = tt // CHUNK

    @pl.when(pl.program_id(2) == 0)
    def _():
        state_scr[...] = jnp.zeros(state_scr.shape, F32)

    def iota(shape, axis):
        return lax.broadcasted_iota(jnp.int32, shape, axis)

    expand = (iota((hg, gw), 1) // hd == iota((hg, gw), 0)).astype(BF16)
    r2, c2 = iota((tt, tt), 0), iota((tt, tt), 1)
    tri = ((c2 <= r2) & (c2 // CHUNK == r2 // CHUNK)).astype(BF16)
    lrow, lcol = iota((CHUNK, gw), 0), iota((CHUNK, gw), 1) % hd
    half = gw // 2
    bd_keep = iota((half * CHUNK // hd, half), 0) // CHUNK == iota((half * CHUNK // hd, half), 1) // hd

    a_row = -jnp.exp(alog_ref[...])
    dt_all = _dot01(expand, dt_ref[...], sel_left=False)
    dta_all = dt_all * a_row
    acum_all = _dot01(tri, dta_all, sel_left=True)

    for ci in range(nch):
        rows = slice(ci * CHUNK, (ci + 1) * CHUNK)
        x = x_ref[rows, :].astype(F32)
        bm = b_ref[rows, :]
        cm = c_ref[rows, :]
        dt, dta, acum = dt_all[rows], dta_all[rows], acum_all[rows]
        r = jnp.sum(jnp.where(lrow <= lcol, dta, 0.0), axis=0, keepdims=True)
        decay = jnp.exp(jnp.where(lrow >= lcol, acum - r, -jnp.inf))
        cb = lax.dot_general(cm, bm, (((1,), (1,)), ((), ())), preferred_element_type=F32)
        m = (jnp.concatenate([cb] * hg, axis=1) * decay).astype(BF16)
        xdt = x * dt
        xdt16 = xdt.astype(BF16)
        ydiag = []
        for hf in range(2):
            cols = slice(hf * half, (hf + 1) * half)
            blockdiag = jnp.where(bd_keep, jnp.concatenate([xdt16[:, cols]] * (half // hd), axis=0), jnp.zeros((), BF16))
            ydiag.append(jnp.dot(m[:, cols], blockdiag, preferred_element_type=F32))
        last = acum[CHUNK - 1:CHUNK, :]
        xw = (xdt * jnp.exp(last - acum)).astype(BF16)
        st = lax.dot_general(bm, xw, (((0,), (0,)), ((), ())), preferred_element_type=F32)
        prev = state_scr[...]
        yoff = jnp.dot(cm, prev.astype(BF16), preferred_element_type=F32) * jnp.exp(acum)
        state_scr[...] = prev * jnp.exp(last) + st
        y = jnp.concatenate(ydiag, axis=1) + yoff + x * dskip_ref[...]
        gated = y * _silu(z_ref[rows, :].astype(F32))
        o_ref[rows, :] = (_rms(gated) * nw_ref[...]).astype(o_ref.dtype)


def _ssd_scan(z, xbc, dt_g, a_log, d_skip, norm_w, bsz, seq, *, tt=256):
    t, d_inner = z.shape
    groups, _, hg = dt_g.shape
    gw = d_inner // groups
    n = SSM_STATE
    nt = seq // tt
    boff = d_inner // n
    coff = boff + groups
    row = lambda b, g, i: b * nt + i
    blocks = [2 * _nbytes((tt, gw), BF16), 2 * _nbytes((tt, n), BF16), _nbytes((tt, LANES), F32), _nbytes((tt, gw), BF16)]
    return pl.pallas_call(
        functools.partial(_ssd_kernel, tt=tt),
        out_shape=jax.ShapeDtypeStruct((t, d_inner), BF16),
        grid=(bsz, groups, nt),
        in_specs=[
            pl.BlockSpec((tt, gw), lambda b, g, i: (row(b, g, i), g)),
            pl.BlockSpec((tt, gw), lambda b, g, i: (row(b, g, i), g)),
            pl.BlockSpec((tt, n), lambda b, g, i: (row(b, g, i), boff + g)),
            pl.BlockSpec((tt, n), lambda b, g, i: (row(b, g, i), coff + g)),
            pl.BlockSpec((None, tt, hg), lambda b, g, i: (g, row(b, g, i), 0)),
            pl.BlockSpec((None, 1, gw), lambda b, g, i: (g, 0, 0)),
            pl.BlockSpec((None, 1, gw), lambda b, g, i: (g, 0, 0)),
            pl.BlockSpec((1, gw), lambda b, g, i: (0, g)),
        ],
        out_specs=pl.BlockSpec((tt, gw), lambda b, g, i: (row(b, g, i), g)),
        scratch_shapes=[pltpu.VMEM((n, gw), F32)],
        compiler_params=_params(("arbitrary", "arbitrary", "arbitrary"),
                                _vmem_limit(blocks, _nbytes((n, gw), F32), 24 * _nbytes((tt, gw), F32))),
        name="ssd_scan",
    )(z, xbc, xbc, xbc, dt_g, a_log, d_skip, norm_w.reshape(1, d_inner))


def _pad_heads(w, heads, width, padded):
    r = w.shape[0]
    w = w.reshape(r, heads, width)
    return jnp.pad(w, ((0, 0), (0, 0), (0, padded - width))).reshape(r, heads * padded)


def kernel(x, c, positions, ada_w, ada_b, norm_pre, norm_post, mla_w_down, mla_q_norm, mla_w_uq, mla_kv_norm, mla_w_ukv, mla_w_o, conv_w_in, conv_w, conv_w_out, ssm_w_in, ssm_conv_w, ssm_conv_b, ssm_dt_bias, ssm_a_log, ssm_d, ssm_norm, ssm_w_out, mlp_up, mlp_down):
    bsz, seq, d = x.shape
    depth = ada_w.shape[0]
    t = bsz * seq
    xf = x.reshape(t, d)

    mods = _ada_all(c, ada_w, ada_b).reshape(depth, 2, bsz, 3, d)
    cos_t, sin_t = _rope_tables(positions)
    mlp_down16 = mlp_down.astype(BF16)

    h = _norm_mod(xf, mods[0, 0], norm_pre[0, 0], seq)
    for i in range(depth):
        kind, j = i % N_MIXERS, i // N_MIXERS
        if kind == 0:
            w_uq = _pad_heads(mla_w_uq[j], MLA_HEADS, MLA_NOPE + MLA_ROPE, HEAD_PAD).astype(BF16)
            cq, ckv, kr = _mla_down(h, mla_w_down, j, mla_q_norm[j], mla_kv_norm[j], cos_t, sin_t)
            q = _q_up(cq, w_uq, cos_t, sin_t, bsz, seq)
            k, vt = _kv_up(ckv, mla_w_ukv[j].astype(BF16), kr, bsz, seq)
            a = _attention(q, k, vt)
            w_o = mla_w_o
        elif kind == 1:
            a = _conv_in(h, conv_w_in, j, conv_w[j], seq)
            w_o = conv_w_out
        else:
            d_inner = ssm_w_out.shape[1]
            nh = ssm_dt_bias.shape[1]
            n_main = ssm_w_in.shape[2] - nh
            z = _mm_w(h, ssm_w_in, j, epilogue=_identity, ncols=d_inner)
            xbc = _ssd_xbc(h, ssm_w_in, j, ssm_conv_w[j], ssm_conv_b[j], d_inner, seq)
            dt = _ssd_dt(h, ssm_w_in[j, :, n_main:].astype(BF16), ssm_dt_bias[j])
            hg = nh // SSM_GROUPS
            gw = d_inner // SSM_GROUPS
            dt_g = dt.reshape(t, SSM_GROUPS, hg).transpose(1, 0, 2)
            per_lane = lambda p: jnp.repeat(p.reshape(SSM_GROUPS, hg), SSM_HEAD_DIM, axis=1).reshape(SSM_GROUPS, 1, gw)
            a = _ssd_scan(z, xbc, dt_g, per_lane(ssm_a_log[j]), per_lane(ssm_d[j]), ssm_norm[j], bsz, seq)
            w_o = ssm_w_out
        xf, h = _mm_resid(a, w_o.astype(BF16), j, xf, mods[i, 0], norm_post[i, 0], (mods[i, 1], norm_pre[i, 1]), seq)
        u = _mm_w(h, mlp_up, i, epilogue=_relu2)
        nxt = (mods[i + 1, 0], norm_pre[i + 1, 0]) if i + 1 < depth else None
        xf, h = _mm_resid(u, mlp_down16, i, xf, mods[i, 1], norm_post[i, 1], nxt, seq)
    return xf.reshape(bsz, seq, d)
```

```python
import functools
import math

import jax
import jax.numpy as jnp
from jax import lax
from jax.experimental import pallas as pl
from jax.experimental.pallas import tpu as pltpu

F32 = jnp.float32
BF16 = jnp.bfloat16

EPS = 1e-6
CHUNK = 64
N_MIXERS = 3
MLA_HEADS = 16
MLA_LORA = 512
MLA_NOPE = 128
MLA_ROPE = 64
MLA_V = 128
ROPE_THETA = 10000.0
SSM_HEAD_DIM = 64
SSM_GROUPS = 8
SSM_STATE = 128

LANES = 128
SUBLANES = 8
V7X_VMEM_BYTES = 64 * 1024 * 1024
HEAD_PAD = 2 * LANES


def _vmem_limit(block_bytes, scratch_bytes=0, temp_bytes=0):
    need = 2 * sum(block_bytes) + scratch_bytes + temp_bytes + (4 << 20)
    return int(min(need, V7X_VMEM_BYTES - (6 << 20)))


def _nbytes(shape, dtype):
    return math.prod(shape) * jnp.dtype(dtype).itemsize


def _params(sem, limit):
    return pltpu.CompilerParams(dimension_semantics=sem, vmem_limit_bytes=limit)


def _rms(x):
    return x * lax.rsqrt(jnp.mean(x * x, axis=-1, keepdims=True) + EPS)


def _modulated_norm(x, mod_ref, g_ref):
    h = _rms(x) * g_ref[...]
    return h * (1.0 + mod_ref[1:2, :]) + mod_ref[0:1, :]


def _silu(x):
    return x * (1.0 / (1.0 + jnp.exp(-x)))


def _ada_kernel(c_ref, w_ref, b_ref, o_ref):
    c_act = _silu(c_ref[...]).astype(BF16)
    y = jnp.dot(c_act, w_ref[...].astype(BF16), preferred_element_type=F32)
    o_ref[...] = y + b_ref[...]


def _ada_all(c, ada_w, ada_b):
    nsub = ada_w.shape[0] * ada_w.shape[1]
    bsz, d = c.shape
    n = ada_w.shape[-1]
    w = ada_w.reshape(nsub, d, n)
    b = ada_b.reshape(nsub, 1, n)
    tn = 1024
    blocks = [_nbytes((d, tn), F32), _nbytes((bsz, tn), F32)]
    return pl.pallas_call(
        _ada_kernel,
        out_shape=jax.ShapeDtypeStruct((nsub, bsz, n), F32),
        grid=(nsub, n // tn),
        in_specs=[
            pl.BlockSpec((bsz, d), lambda s, j: (0, 0)),
            pl.BlockSpec((None, d, tn), lambda s, j: (s, 0, j)),
            pl.BlockSpec((None, 1, tn), lambda s, j: (s, 0, j)),
        ],
        out_specs=pl.BlockSpec((None, bsz, tn), lambda s, j: (s, 0, j)),
        compiler_params=_params(("arbitrary", "arbitrary"), _vmem_limit(blocks, temp_bytes=_nbytes((d, tn), BF16))),
        name="ada_mod",
    )(c, w, b)


def _rope_kernel(pos_ref, freq_ref, cos_ref, sin_ref):
    ang = pos_ref[...] * freq_ref[...]
    lane = lax.broadcasted_iota(jnp.int32, ang.shape, 1)
    half = MLA_ROPE // 2
    cos_ref[...] = jnp.where(lane < MLA_ROPE, jnp.cos(ang), 0.0)
    s = jnp.sin(ang)
    sin_ref[...] = jnp.where(lane < half, -s, jnp.where(lane < MLA_ROPE, s, 0.0))


def _rope_tables(positions):
    t = positions.size
    half = MLA_ROPE // 2
    inv_freq = ROPE_THETA ** (-jnp.arange(0, MLA_ROPE, 2, dtype=F32) / MLA_ROPE)
    freq = jnp.concatenate([inv_freq, inv_freq, jnp.zeros((LANES - 2 * half,), F32)]).reshape(1, LANES)
    pos = positions.astype(F32).reshape(t, 1)
    tm = min(t, 1024)
    out = jax.ShapeDtypeStruct((t, LANES), F32)
    return pl.pallas_call(
        _rope_kernel,
        out_shape=(out, out),
        grid=(t // tm,),
        in_specs=[pl.BlockSpec((tm, 1), lambda i: (i, 0)), pl.BlockSpec((1, LANES), lambda i: (0, 0))],
        out_specs=(pl.BlockSpec((tm, LANES), lambda i: (i, 0)), pl.BlockSpec((tm, LANES), lambda i: (i, 0))),
        compiler_params=_params(("arbitrary",), _vmem_limit([_nbytes((tm, LANES), F32)] * 3, temp_bytes=8 << 20)),
        name="rope_tables",
    )(pos, freq)


def _rope_slab(z, cos_ref, sin_ref):
    half = MLA_ROPE // 2
    lane = lax.broadcasted_iota(jnp.int32, z.shape, 1)
    partner = jnp.where(lane < half, pltpu.roll(z, LANES - half, axis=1), pltpu.roll(z, half, axis=1))
    return z * cos_ref[...] + partner * sin_ref[...]


def _relu2(y):
    r = jnp.maximum(y, 0.0)
    return r * r


def _norm_mod_kernel(x_ref, mod_ref, g_ref, h_ref):
    h_ref[...] = _modulated_norm(x_ref[...], mod_ref, g_ref).astype(BF16)


def _norm_mod(x, mod, g, seq, *, tm=1024):
    t, d = x.shape
    tiles_per_batch = seq // tm
    blocks = [_nbytes((tm, d), F32), _nbytes((tm, d), BF16)]
    return pl.pallas_call(
        _norm_mod_kernel,
        out_shape=jax.ShapeDtypeStruct((t, d), BF16),
        grid=(t // tm,),
        in_specs=[
            pl.BlockSpec((tm, d), lambda i: (i, 0)),
            pl.BlockSpec((None, 3, d), lambda i: (i // tiles_per_batch, 0, 0)),
            pl.BlockSpec((1, d), lambda i: (0, 0)),
        ],
        out_specs=pl.BlockSpec((tm, d), lambda i: (i, 0)),
        compiler_params=_params(("arbitrary",), _vmem_limit(blocks, temp_bytes=2 * _nbytes((tm, d), F32))),
        name="norm_mod",
    )(x, mod, g.reshape(1, d))


def _mm_w_kernel(*refs, epilogue, side_cast):
    if side_cast:
        h_ref, w_ref, c_ref, o_ref, c16_ref, w_scr = refs
        c16_ref[...] = c_ref[...].astype(BF16)
    else:
        h_ref, w_ref, o_ref, w_scr = refs

    @pl.when(pl.program_id(1) == 0)
    def _():
        w_scr[...] = w_ref[...].astype(BF16)

    y = jnp.dot(h_ref[...], w_scr[...], preferred_element_type=F32)
    o_ref[...] = epilogue(y).astype(o_ref.dtype)


def _identity(y):
    return y


def _mm_w(h, w_stack, layer, *, epilogue, ncols=None, cast_stack=None, tm=1024, tn=1024):
    t, d = h.shape
    n = w_stack.shape[2] if ncols is None else ncols
    nj, ni = n // tn, t // tm
    blocks = [_nbytes((tm, d), BF16), _nbytes((d, tn), F32), _nbytes((tm, tn), BF16)]
    in_specs = [
        pl.BlockSpec((tm, d), lambda j, i: (i, 0)),
        pl.BlockSpec((None, d, tn), lambda j, i: (layer, 0, j)),
    ]
    args = [h, w_stack]
    out_shape = [jax.ShapeDtypeStruct((t, n), BF16)]
    out_specs = [pl.BlockSpec((tm, tn), lambda j, i: (i, j))]
    if cast_stack is not None:
        rows, cols = cast_stack.shape[1:]
        assert rows % (nj * ni * 2 * SUBLANES) == 0, "side-cast slabs must tile the weight rows exactly"
        slab = rows // (nj * ni)
        in_specs.append(pl.BlockSpec((None, slab, cols), lambda j, i: (layer, j * ni + i, 0)))
        args.append(cast_stack)
        out_shape.append(jax.ShapeDtypeStruct((rows, cols), BF16))
        out_specs.append(pl.BlockSpec((slab, cols), lambda j, i: (j * ni + i, 0)))
        blocks += [_nbytes((slab, cols), F32), _nbytes((slab, cols), BF16)]
    outs = pl.pallas_call(
        functools.partial(_mm_w_kernel, epilogue=epilogue, side_cast=cast_stack is not None),
        out_shape=tuple(out_shape),
        grid=(nj, ni),
        in_specs=in_specs,
        out_specs=tuple(out_specs),
        scratch_shapes=[pltpu.VMEM((d, tn), BF16)],
        compiler_params=_params(("arbitrary", "arbitrary"),
                                _vmem_limit(blocks, _nbytes((d, tn), BF16), 2 * _nbytes((tm, tn), F32))),
        name="mm_w",
    )(*args)
    return outs if cast_stack is not None else outs[0]


_EPILOGUE_ROWS = 256


def _mm_resid_kernel(*refs, nk, emit_next):
    a_ref, w_ref, x_ref, mod_ref, g_ref = refs[:5]
    refs = refs[5:]
    if emit_next:
        modn_ref, gn_ref, o_ref, hn_ref = refs[:4]
        refs = refs[4:]
    else:
        o_ref = refs[0]
        refs = refs[1:]
    i, k = pl.program_id(0), pl.program_id(1)
    tm = o_ref.shape[0]

    if nk > 1:
        x_scr, x_sem = refs

        def x_copy():
            return pltpu.make_async_copy(x_ref.at[pl.ds(pl.multiple_of(i * tm, tm), tm), :], x_scr, x_sem)

        def partial_product(rows):
            return jnp.dot(a_ref[rows, :], w_ref[...], preferred_element_type=F32)

        @pl.when(k == 0)
        def _():
            x_copy().start()
            for r0 in range(0, tm, _EPILOGUE_ROWS):
                rows = slice(r0, r0 + _EPILOGUE_ROWS)
                o_ref[rows, :] = partial_product(rows)

        if nk > 2:
            @pl.when((k > 0) & (k < nk - 1))
            def _():
                for r0 in range(0, tm, _EPILOGUE_ROWS):
                    rows = slice(r0, r0 + _EPILOGUE_ROWS)
                    o_ref[rows, :] += partial_product(rows)
    else:
        x_scr = x_ref

    def finish():
        gate_g = mod_ref[2:3, :] * g_ref[...]
        if emit_next:
            scale_g = gn_ref[...] * (1.0 + modn_ref[1:2, :])
        for r0 in range(0, tm, _EPILOGUE_ROWS):
            rows = slice(r0, r0 + _EPILOGUE_ROWS)
            y = jnp.dot(a_ref[rows, :], w_ref[...], preferred_element_type=F32)
            if nk > 1:
                y = o_ref[rows, :] + y
            x_new = x_scr[rows, :] + _rms(y) * gate_g
            o_ref[rows, :] = x_new
            if emit_next:
                hn_ref[rows, :] = (_rms(x_new) * scale_g + modn_ref[0:1, :]).astype(BF16)

    if nk > 1:
        @pl.when(k == nk - 1)
        def _():
            x_copy().wait()
            finish()
    else:
        finish()


def _mm_resid(a, w_stack, layer, x, mod, g, nxt, seq):
    t, kdim = a.shape
    d = w_stack.shape[2]
    tm, tk = (1024, 1024) if kdim >= 4 * d else (512, min(kdim, 2048))
    nk = kdim // tk
    tiles_per_batch = seq // tm
    emit_next = nxt is not None
    batch_map = lambda i, k: (i // tiles_per_batch, 0, 0)
    fixed = lambda i, k: (0, 0)
    x_spec = pl.BlockSpec((tm, d), lambda i, k: (i, 0)) if nk == 1 else pl.BlockSpec(memory_space=pl.ANY)
    in_specs = [
        pl.BlockSpec((tm, tk), lambda i, k: (i, k)),
        pl.BlockSpec((None, tk, d), lambda i, k: (layer, k, 0)),
        x_spec,
        pl.BlockSpec((None, 3, d), batch_map),
        pl.BlockSpec((1, d), fixed),
    ]
    args = [a, w_stack, x, mod, g.reshape(1, d)]
    out_shape = [jax.ShapeDtypeStruct((t, d), F32)]
    out_specs = [pl.BlockSpec((tm, d), lambda i, k: (i, 0))]
    blocks = [_nbytes((tm, tk), BF16), _nbytes((tk, d), BF16), _nbytes((tm, d), F32)]
    if emit_next:
        in_specs += [pl.BlockSpec((None, 3, d), batch_map), pl.BlockSpec((1, d), fixed)]
        args += [nxt[0], nxt[1].reshape(1, d)]
        out_shape.append(jax.ShapeDtypeStruct((t, d), BF16))
        out_specs.append(pl.BlockSpec((tm, d), lambda i, k: (i, 0)))
        blocks.append(_nbytes((tm, d), BF16))
    if nk == 1:
        scratch, scratch_bytes = [], 0
        blocks.append(_nbytes((tm, d), F32))
    else:
        scratch, scratch_bytes = [pltpu.VMEM((tm, d), F32), pltpu.SemaphoreType.DMA(())], _nbytes((tm, d), F32)
    outs = pl.pallas_call(
        functools.partial(_mm_resid_kernel, nk=nk, emit_next=emit_next),
        out_shape=tuple(out_shape),
        grid=(t // tm, nk),
        in_specs=in_specs,
        out_specs=tuple(out_specs),
        scratch_shapes=scratch,
        compiler_params=_params(("arbitrary", "arbitrary"),
                                _vmem_limit(blocks, scratch_bytes, 8 * _nbytes((_EPILOGUE_ROWS, d), F32))),
        name="mm_resid",
    )(*args)
    return (outs[0], outs[1]) if emit_next else (outs[0], None)


def _mla_down_kernel(h_ref, w_ref, qn_ref, kvn_ref, cos_ref, sin_ref, cq_ref, ckv_ref, kr_ref, w_scr):
    @pl.when(pl.program_id(0) == 0)
    def _():
        w_scr[...] = w_ref[...].astype(BF16)

    y = jnp.dot(h_ref[...], w_scr[...], preferred_element_type=F32)
    cq_ref[...] = (_rms(y[:, :MLA_LORA]) * qn_ref[...]).astype(BF16)
    ckv_ref[...] = (_rms(y[:, MLA_LORA:2 * MLA_LORA]) * kvn_ref[...]).astype(BF16)
    slab = jnp.concatenate([y[:, 2 * MLA_LORA:], jnp.zeros((y.shape[0], LANES - MLA_ROPE), F32)], axis=1)
    kr_ref[...] = _rope_slab(slab, cos_ref, sin_ref).astype(BF16)


def _mla_down(h, w_stack, layer, q_norm, kv_norm, cos_t, sin_t, *, tm=512):
    t, d = h.shape
    n = w_stack.shape[2]
    blocks = [_nbytes((tm, d), BF16), _nbytes((d, n), F32), 3 * _nbytes((tm, MLA_LORA), BF16), 2 * _nbytes((tm, LANES), F32)]
    row = lambda i: (i, 0)
    fixed = lambda i: (0, 0)
    return pl.pallas_call(
        _mla_down_kernel,
        out_shape=(jax.ShapeDtypeStruct((t, MLA_LORA), BF16), jax.ShapeDtypeStruct((t, MLA_LORA), BF16),
                   jax.ShapeDtypeStruct((t, LANES), BF16)),
        grid=(t // tm,),
        in_specs=[
            pl.BlockSpec((tm, d), row),
            pl.BlockSpec((None, d, n), lambda i: (layer, 0, 0)),
            pl.BlockSpec((1, MLA_LORA), fixed),
            pl.BlockSpec((1, MLA_LORA), fixed),
            pl.BlockSpec((tm, LANES), row),
            pl.BlockSpec((tm, LANES), row),
        ],
        out_specs=(pl.BlockSpec((tm, MLA_LORA), row), pl.BlockSpec((tm, MLA_LORA), row), pl.BlockSpec((tm, LANES), row)),
        scratch_shapes=[pltpu.VMEM((d, n), BF16)],
        compiler_params=_params(("arbitrary",), _vmem_limit(blocks, _nbytes((d, n), BF16), 3 * _nbytes((tm, n), F32))),
        name="mla_down",
    )(h, w_stack, q_norm.reshape(1, -1), kv_norm.reshape(1, -1), cos_t, sin_t)


def _q_up_kernel(a_ref, w_ref, cos_ref, sin_ref, q_ref, *, heads, scale):
    y = jnp.dot(a_ref[...], w_ref[...], preferred_element_type=F32) * scale
    for h in range(heads):
        base = h * HEAD_PAD
        q_ref[h, :, :LANES] = y[:, base:base + LANES].astype(BF16)
        q_ref[h, :, LANES:] = _rope_slab(y[:, base + LANES:base + HEAD_PAD], cos_ref, sin_ref).astype(BF16)


def _kv_up_kernel(a_ref, w_ref, kr_ref, k_ref, vt_ref, *, heads):
    y = jnp.dot(a_ref[...], w_ref[...], preferred_element_type=F32)
    kr = kr_ref[...]
    for h in range(heads):
        base = h * (MLA_NOPE + MLA_V)
        k_ref[h, :, :LANES] = y[:, base:base + MLA_NOPE].astype(BF16)
        k_ref[h, :, LANES:] = kr
        vt_ref[h] = y[:, base + MLA_NOPE:base + MLA_NOPE + MLA_V].T.astype(BF16)


def _q_up(cq, w_uq, cos_t, sin_t, bsz, seq, *, tm=1024, heads=4):
    t, r = cq.shape
    tiles_per_batch = seq // tm
    scale = math.log2(math.e) / math.sqrt(MLA_NOPE + MLA_ROPE)
    tn = heads * HEAD_PAD
    blocks = [_nbytes((tm, r), BF16), _nbytes((r, tn), BF16), 2 * _nbytes((tm, LANES), F32), _nbytes((heads, tm, HEAD_PAD), BF16)]
    return pl.pallas_call(
        functools.partial(_q_up_kernel, heads=heads, scale=scale),
        out_shape=jax.ShapeDtypeStruct((bsz, MLA_HEADS, seq, HEAD_PAD), BF16),
        grid=(t // tm, MLA_HEADS // heads),
        in_specs=[
            pl.BlockSpec((tm, r), lambda i, j: (i, 0)),
            pl.BlockSpec((r, tn), lambda i, j: (0, j)),
            pl.BlockSpec((tm, LANES), lambda i, j: (i, 0)),
            pl.BlockSpec((tm, LANES), lambda i, j: (i, 0)),
        ],
        out_specs=pl.BlockSpec((None, heads, tm, HEAD_PAD),
                               lambda i, j: (i // tiles_per_batch, j, i % tiles_per_batch, 0)),
        compiler_params=_params(("arbitrary", "arbitrary"), _vmem_limit(blocks, temp_bytes=2 * _nbytes((tm, tn), F32))),
        name="mla_q_up",
    )(cq, w_uq, cos_t, sin_t)


def _kv_up(ckv, w_ukv, kr, bsz, seq, *, tm=1024, heads=4):
    t, r = ckv.shape
    tiles_per_batch = seq // tm
    tn = heads * (MLA_NOPE + MLA_V)
    blocks = [_nbytes((tm, r), BF16), _nbytes((r, tn), BF16), _nbytes((tm, LANES), BF16),
              _nbytes((heads, tm, HEAD_PAD), BF16), _nbytes((heads, tm, MLA_V), BF16)]
    out_map = lambda i, j: (i // tiles_per_batch, j, i % tiles_per_batch, 0)
    vt_map = lambda i, j: (i // tiles_per_batch, j, 0, i % tiles_per_batch)
    return pl.pallas_call(
        functools.partial(_kv_up_kernel, heads=heads),
        out_shape=(jax.ShapeDtypeStruct((bsz, MLA_HEADS, seq, HEAD_PAD), BF16),
                   jax.ShapeDtypeStruct((bsz, MLA_HEADS, MLA_V, seq), BF16)),
        grid=(t // tm, MLA_HEADS // heads),
        in_specs=[
            pl.BlockSpec((tm, r), lambda i, j: (i, 0)),
            pl.BlockSpec((r, tn), lambda i, j: (0, j)),
            pl.BlockSpec((tm, LANES), lambda i, j: (i, 0)),
        ],
        out_specs=(pl.BlockSpec((None, heads, tm, HEAD_PAD), out_map), pl.BlockSpec((None, heads, MLA_V, tm), vt_map)),
        compiler_params=_params(("arbitrary", "arbitrary"), _vmem_limit(blocks, temp_bytes=2 * _nbytes((tm, tn), F32))),
        name="mla_kv_up",
    )(ckv, w_ukv, kr)


def _attn_kernel(q_ref, k_ref, vt_ref, o_ref, *scratch, tq, tk, lanes_q):
    def tile(qi, carry):
        _attn_tile(qi, q_ref, k_ref, vt_ref, o_ref, *scratch, tq=tq, tk=tk, lanes_q=lanes_q)
        return carry

    lax.fori_loop(0, q_ref.shape[0] // tq, tile, 0)


def _attn_tile(qi, q_ref, k_ref, vt_ref, o_ref, m_scr, l_scr, acc_scr, sa_scr, sb_scr, bma_scr, bmb_scr, *, tq, tk, lanes_q):
    ngrp = tq // lanes_q

    def q_rows(c):
        return pl.ds(pl.multiple_of(qi * tq + c * lanes_q, lanes_q), lanes_q)

    ndiag = tq // tk
    assert ndiag == 2, "the two-slot score ring below assumes two key blocks per query tile"
    nfull = qi * ndiag
    ring = ((sa_scr, bma_scr), (sb_scr, bmb_scr))
    m_scr[...] = jnp.full(m_scr.shape, -jnp.inf, F32)
    l_scr[...] = jnp.zeros(l_scr.shape, F32)
    acc_scr[...] = jnp.zeros(acc_scr.shape, F32)

    def skipped(c, diag):
        return diag is not None and (c + 1) * lanes_q <= diag * tk

    def scores(blk, diag, slot):
        s_scr, bm_scr = ring[slot]
        kb = k_ref[pl.ds(pl.multiple_of(blk * tk, tk), tk), :]
        for c in range(ngrp):
            if skipped(c, diag):
                continue
            qc = q_ref[q_rows(c), :]
            s = lax.dot_general(kb, qc, (((1,), (1,)), ((), ())), preferred_element_type=F32)
            if diag is not None and c * lanes_q < (diag + 1) * tk:
                k_chunk = (lax.broadcasted_iota(jnp.int32, s.shape, 0) + diag * tk) // CHUNK
                q_chunk = (lax.broadcasted_iota(jnp.int32, s.shape, 1) + c * lanes_q) // CHUNK
                s = jnp.where(k_chunk <= q_chunk, s, -jnp.inf)
            s_scr[c] = s
            bm_scr[c] = jnp.max(s, axis=0, keepdims=True)

    def accumulate(blk, diag, slot):
        s_scr, bm_scr = ring[slot]
        vtb = vt_ref[:, pl.ds(pl.multiple_of(blk * tk, tk), tk)]
        for c in range(ngrp):
            if skipped(c, diag):
                continue
            m_prev = m_scr[c]
            m_new = jnp.maximum(m_prev, bm_scr[c])
            alpha = jnp.exp2(m_prev - m_new)
            p = jnp.exp2(s_scr[c] - m_new)
            l_scr[c] = alpha * l_scr[c] + jnp.sum(p, axis=0, keepdims=True)
            acc_scr[c] = alpha * acc_scr[c] + jnp.dot(vtb, p.astype(BF16), preferred_element_type=F32)
            m_scr[c] = m_new

    @pl.when(qi > 0)
    def _():
        scores(0, None, 0)

    @pl.when(qi == 0)
    def _():
        scores(0, 0, 0)

    def body(i, carry):
        b0 = 2 * i
        scores(b0 + 1, None, 1)
        accumulate(b0, None, 0)
        scores(b0 + 2, None, 0)
        accumulate(b0 + 1, None, 1)
        return carry

    lax.fori_loop(0, qi - 1, body, 0)

    @pl.when(qi > 0)
    def _():
        scores(nfull - 1, None, 1)
        accumulate(nfull - 2, None, 0)
        scores(nfull, 0, 0)
        accumulate(nfull - 1, None, 1)

    scores(nfull + 1, 1, 1)
    accumulate(nfull, 0, 0)
    accumulate(nfull + 1, 1, 1)
    for c in range(ngrp):
        o_ref[q_rows(c), :] = (acc_scr[c] / l_scr[c]).T.astype(o_ref.dtype)


def _attention(q, k, vt, *, tq=1024, tk=512, lanes_q=256):
    bsz, heads, seq, _ = q.shape
    ngrp = tq // lanes_q
    blocks = [_nbytes((seq, HEAD_PAD), BF16), _nbytes((seq, HEAD_PAD), BF16), _nbytes((MLA_V, seq), BF16), _nbytes((seq, MLA_V), BF16)]
    scratch = (2 * _nbytes((ngrp, SUBLANES, lanes_q), F32) + _nbytes((ngrp, MLA_V, lanes_q), F32)
               + 2 * _nbytes((ngrp, tk, lanes_q), F32) + 2 * _nbytes((ngrp, SUBLANES, lanes_q), F32))
    return pl.pallas_call(
        functools.partial(_attn_kernel, tq=tq, tk=tk, lanes_q=lanes_q),
        out_shape=jax.ShapeDtypeStruct((bsz * seq, heads * MLA_V), BF16),
        grid=(bsz, heads),
        in_specs=[
            pl.BlockSpec((None, None, seq, HEAD_PAD), lambda b, h: (b, h, 0, 0)),
            pl.BlockSpec((None, None, seq, HEAD_PAD), lambda b, h: (b, h, 0, 0)),
            pl.BlockSpec((None, None, MLA_V, seq), lambda b, h: (b, h, 0, 0)),
        ],
        out_specs=pl.BlockSpec((seq, MLA_V), lambda b, h: (b, h)),
        scratch_shapes=[pltpu.VMEM((ngrp, 1, lanes_q), F32), pltpu.VMEM((ngrp, 1, lanes_q), F32),
                        pltpu.VMEM((ngrp, MLA_V, lanes_q), F32),
                        pltpu.VMEM((ngrp, tk, lanes_q), F32), pltpu.VMEM((ngrp, tk, lanes_q), F32),
                        pltpu.VMEM((ngrp, 1, lanes_q), F32), pltpu.VMEM((ngrp, 1, lanes_q), F32)],
        compiler_params=_params(("arbitrary", "arbitrary"),
                                _vmem_limit(blocks, scratch, 8 * _nbytes((tk, tq), F32))),
        name="mla_attention",
    )(q, k, vt)


def _causal_conv(t, tail_ref, buf_ref, w_ref, first_of_batch):
    kw = w_ref.shape[0]
    tm = t.shape[0]
    hist = jnp.where(first_of_batch, 0.0, tail_ref[...])
    buf_ref[pl.ds(0, SUBLANES), :] = hist
    buf_ref[pl.ds(SUBLANES, tm), :] = t
    tail_ref[...] = t[tm - SUBLANES:, :]
    acc = t * w_ref[kw - 1:kw, :]
    for k in range(kw - 1):
        acc = acc + buf_ref[pl.ds(SUBLANES - (kw - 1) + k, tm), :] * w_ref[k:k + 1, :]
    return acc


def _conv_in_kernel(h_ref, wb_ref, wc_ref, wu_ref, cw_ref, o_ref, wb_scr, wc_scr, wu_scr, tail_scr, buf_scr, *, tiles_per_batch):
    i = pl.program_id(1)

    @pl.when(i == 0)
    def _():
        wb_scr[...] = wb_ref[...].astype(BF16)
        wc_scr[...] = wc_ref[...].astype(BF16)
        wu_scr[...] = wu_ref[...].astype(BF16)

    h = h_ref[...]
    gc = jnp.dot(h, wc_scr[...], preferred_element_type=F32)
    u = jnp.dot(h, wu_scr[...], preferred_element_type=F32)
    conv = _causal_conv(gc * u, tail_scr, buf_scr, cw_ref, i % tiles_per_batch == 0)
    gb = jnp.dot(h, wb_scr[...], preferred_element_type=F32)
    o_ref[...] = (gb * conv).astype(o_ref.dtype)


def _conv_in(h, w_stack, layer, conv_w, seq, *, tm=512, tn=512):
    t, d = h.shape
    n = w_stack.shape[2] // 3
    nj = n // tn
    tiles_per_batch = seq // tm
    kw = conv_w.shape[0]
    blocks = [_nbytes((tm, d), BF16), 3 * _nbytes((d, tn), F32), _nbytes((tm, tn), BF16)]
    scratch = 3 * _nbytes((d, tn), BF16) + _nbytes((SUBLANES, tn), F32) + _nbytes((tm + SUBLANES, tn), F32)
    wspec = lambda off: pl.BlockSpec((None, d, tn), lambda j, i: (layer, 0, j + off * nj))
    return pl.pallas_call(
        functools.partial(_conv_in_kernel, tiles_per_batch=tiles_per_batch),
        out_shape=jax.ShapeDtypeStruct((t, n), BF16),
        grid=(nj, t // tm),
        in_specs=[
            pl.BlockSpec((tm, d), lambda j, i: (i, 0)),
            wspec(0), wspec(1), wspec(2),
            pl.BlockSpec((kw, tn), lambda j, i: (0, j)),
        ],
        out_specs=pl.BlockSpec((tm, tn), lambda j, i: (i, j)),
        scratch_shapes=[pltpu.VMEM((d, tn), BF16), pltpu.VMEM((d, tn), BF16), pltpu.VMEM((d, tn), BF16),
                        pltpu.VMEM((SUBLANES, tn), F32), pltpu.VMEM((tm + SUBLANES, tn), F32)],
        compiler_params=_params(("arbitrary", "arbitrary"), _vmem_limit(blocks, scratch, 6 * _nbytes((tm, tn), F32))),
        name="conv_in",
    )(h, w_stack, w_stack, w_stack, conv_w)


def _softplus(x):
    return jnp.maximum(x, 0.0) + jnp.log1p(jnp.exp(-jnp.abs(x)))


def _ssd_xbc_kernel(h_ref, w_ref, cw_ref, cb_ref, o_ref, w_scr, tail_scr, buf_scr, *, tiles_per_batch):
    i = pl.program_id(1)

    @pl.when(i == 0)
    def _():
        w_scr[...] = w_ref[...].astype(BF16)

    y = jnp.dot(h_ref[...], w_scr[...], preferred_element_type=F32)
    conv = _causal_conv(y, tail_scr, buf_scr, cw_ref, i % tiles_per_batch == 0) + cb_ref[...]
    o_ref[...] = _silu(conv).astype(o_ref.dtype)


def _ssd_xbc(h, w_stack, layer, conv_w, conv_b, col0, seq, *, tm=512, tn=1024):
    t, d = h.shape
    kw, n = conv_w.shape
    tiles_per_batch = seq // tm
    joff = col0 // tn
    blocks = [_nbytes((tm, d), BF16), _nbytes((d, tn), F32), _nbytes((tm, tn), BF16)]
    scratch = _nbytes((d, tn), BF16) + _nbytes((SUBLANES, tn), F32) + _nbytes((tm + SUBLANES, tn), F32)
    return pl.pallas_call(
        functools.partial(_ssd_xbc_kernel, tiles_per_batch=tiles_per_batch),
        out_shape=jax.ShapeDtypeStruct((t, n), BF16),
        grid=(n // tn, t // tm),
        in_specs=[
            pl.BlockSpec((tm, d), lambda j, i: (i, 0)),
            pl.BlockSpec((None, d, tn), lambda j, i: (layer, 0, j + joff)),
            pl.BlockSpec((kw, tn), lambda j, i: (0, j)),
            pl.BlockSpec((1, tn), lambda j, i: (0, j)),
        ],
        out_specs=pl.BlockSpec((tm, tn), lambda j, i: (i, j)),
        scratch_shapes=[pltpu.VMEM((d, tn), BF16), pltpu.VMEM((SUBLANES, tn), F32), pltpu.VMEM((tm + SUBLANES, tn), F32)],
        compiler_params=_params(("arbitrary", "arbitrary"), _vmem_limit(blocks, scratch, 6 * _nbytes((tm, tn), F32))),
        name="ssd_xbc",
    )(h, w_stack, conv_w, conv_b.reshape(1, -1))


def _ssd_dt_kernel(h_ref, w_ref, b_ref, dt_ref):
    dt_ref[...] = _softplus(jnp.dot(h_ref[...], w_ref[...], preferred_element_type=F32) + b_ref[...])


def _ssd_dt(h, w_dt, dt_bias, *, tm=1024):
    t, d = h.shape
    nh = w_dt.shape[1]
    blocks = [_nbytes((tm, d), BF16), _nbytes((d, LANES), BF16), _nbytes((tm, LANES), F32)]
    return pl.pallas_call(
        _ssd_dt_kernel,
        out_shape=jax.ShapeDtypeStruct((t, nh), F32),
        grid=(t // tm,),
        in_specs=[
            pl.BlockSpec((tm, d), lambda i: (i, 0)),
            pl.BlockSpec((d, nh), lambda i: (0, 0)),
            pl.BlockSpec((1, nh), lambda i: (0, 0)),
        ],
        out_specs=pl.BlockSpec((tm, nh), lambda i: (i, 0)),
        compiler_params=_params(("arbitrary",), _vmem_limit(blocks, temp_bytes=4 * _nbytes((tm, LANES), F32))),
        name="ssd_dt",
    )(h, w_dt, dt_bias.reshape(1, nh))


def _split3(x):
    hi = x.astype(BF16)
    r1 = x - hi.astype(F32)
    mid = r1.astype(BF16)
    lo = (r1 - mid.astype(F32)).astype(BF16)
    return hi, mid, lo


def _dot01(sel, x, *, sel_left):
    out = None
    for part in _split3(x):
        d = (jnp.dot(sel, part, preferred_element_type=F32) if sel_left
             else jnp.dot(part, sel, preferred_element_type=F32))
        out = d if out is None else out + d
    return out


def _ssd_kernel(z_ref, x_ref, b_ref, c_ref, dt_ref, alog_ref, dskip_ref, nw_ref, o_ref, state_scr, *, tt):
    hg = dt_ref.shape[-1]
    gw = x_ref.shape[-1]
    hd = gw // hg
    nch = tt // CHUNK

    @pl.when(pl.program_id(2) == 0)
    def _():
        state_scr[...] = jnp.zeros(state_scr.shape, F32)

    def iota(shape, axis):
        return lax.broadcasted_iota(jnp.int32, shape, axis)

    expand = (iota((hg, gw), 1) // hd == iota((hg, gw), 0)).astype(BF16)
    r2, c2 = iota((tt, tt), 0), iota((tt, tt), 1)
    tri = ((c2 <= r2) & (c2 // CHUNK == r2 // CHUNK)).astype(BF16)
    lrow, lcol = iota((CHUNK, gw), 0), iota((CHUNK, gw), 1) % hd
    half = gw // 2
    bd_keep = iota((half * CHUNK // hd, half), 0) // CHUNK == iota((half * CHUNK // hd, half), 1) // hd

    a_row = -jnp.exp(alog_ref[...])
    dt_all = _dot01(expand, dt_ref[...], sel_left=False)
    dta_all = dt_all * a_row
    acum_all = _dot01(tri, dta_all, sel_left=True)

    for ci in range(nch):
        rows = slice(ci * CHUNK, (ci + 1) * CHUNK)
        x = x_ref[rows, :].astype(F32)
        bm = b_ref[rows, :]
        cm = c_ref[rows, :]
        dt, dta, acum = dt_all[rows], dta_all[rows], acum_all[rows]
        r = jnp.sum(jnp.where(lrow <= lcol, dta, 0.0), axis=0, keepdims=True)
        decay = jnp.exp(jnp.where(lrow >= lcol, acum - r, -jnp.inf))
        cb = lax.dot_general(cm, bm, (((1,), (1,)), ((), ())), preferred_element_type=F32)
        m = (jnp.concatenate([cb] * hg, axis=1) * decay).astype(BF16)
        xdt = x * dt
        xdt16 = xdt.astype(BF16)
        ydiag = []
        for hf in range(2):
            cols = slice(hf * half, (hf + 1) * half)
            blockdiag = jnp.where(bd_keep, jnp.concatenate([xdt16[:, cols]] * (half // hd), axis=0), jnp.zeros((), BF16))
            ydiag.append(jnp.dot(m[:, cols], blockdiag, preferred_element_type=F32))
        last = acum[CHUNK - 1:CHUNK, :]
        xw = (xdt * jnp.exp(last - acum)).astype(BF16)
        st = lax.dot_general(bm, xw, (((0,), (0,)), ((), ())), preferred_element_type=F32)
        prev = state_scr[...]
        yoff = jnp.dot(cm, prev.astype(BF16), preferred_element_type=F32) * jnp.exp(acum)
        state_scr[...] = prev * jnp.exp(last) + st
        y = jnp.concatenate(ydiag, axis=1) + yoff + x * dskip_ref[...]
        gated = y * _silu(z_ref[rows, :].astype(F32))
        o_ref[rows, :] = (_rms(gated) * nw_ref[...]).astype(o_ref.dtype)


def _ssd_scan(z, xbc, dt_g, a_log, d_skip, norm_w, bsz, seq, *, tt=256):
    t, d_inner = z.shape
    groups, _, hg = dt_g.shape
    gw = d_inner // groups
    n = SSM_STATE
    nt = seq // tt
    boff = d_inner // n
    coff = boff + groups
    row = lambda b, g, i: b * nt + i
    blocks = [2 * _nbytes((tt, gw), BF16), 2 * _nbytes((tt, n), BF16), _nbytes((tt, LANES), F32), _nbytes((tt, gw), BF16)]
    return pl.pallas_call(
        functools.partial(_ssd_kernel, tt=tt),
        out_shape=jax.ShapeDtypeStruct((t, d_inner), BF16),
        grid=(bsz, groups, nt),
        in_specs=[
            pl.BlockSpec((tt, gw), lambda b, g, i: (row(b, g, i), g)),
            pl.BlockSpec((tt, gw), lambda b, g, i: (row(b, g, i), g)),
            pl.BlockSpec((tt, n), lambda b, g, i: (row(b, g, i), boff + g)),
            pl.BlockSpec((tt, n), lambda b, g, i: (row(b, g, i), coff + g)),
            pl.BlockSpec((None, tt, hg), lambda b, g, i: (g, row(b, g, i), 0)),
            pl.BlockSpec((None, 1, gw), lambda b, g, i: (g, 0, 0)),
            pl.BlockSpec((None, 1, gw), lambda b, g, i: (g, 0, 0)),
            pl.BlockSpec((1, gw), lambda b, g, i: (0, g)),
        ],
        out_specs=pl.BlockSpec((tt, gw), lambda b, g, i: (row(b, g, i), g)),
        scratch_shapes=[pltpu.VMEM((n, gw), F32)],
        compiler_params=_params(("arbitrary", "arbitrary", "arbitrary"),
                                _vmem_limit(blocks, _nbytes((n, gw), F32), 24 * _nbytes((tt, gw), F32))),
        name="ssd_scan",
    )(z, xbc, xbc, xbc, dt_g, a_log, d_skip, norm_w.reshape(1, d_inner))


def _pad_heads(w, heads, width, padded):
    r = w.shape[0]
    w = w.reshape(r, heads, width)
    return jnp.pad(w, ((0, 0), (0, 0), (0, padded - width))).reshape(r, heads * padded)


def kernel(x, c, positions, ada_w, ada_b, norm_pre, norm_post, mla_w_down, mla_q_norm, mla_w_uq, mla_kv_norm, mla_w_ukv, mla_w_o, conv_w_in, conv_w, conv_w_out, ssm_w_in, ssm_conv_w, ssm_conv_b, ssm_dt_bias, ssm_a_log, ssm_d, ssm_norm, ssm_w_out, mlp_up, mlp_down):
    bsz, seq, d = x.shape
    depth = ada_w.shape[0]
    t = bsz * seq
    xf = x.reshape(t, d)

    mods = _ada_all(c, ada_w, ada_b).reshape(depth, 2, bsz, 3, d)
    cos_t, sin_t = _rope_tables(positions)

    h = _norm_mod(xf, mods[0, 0], norm_pre[0, 0], seq)
    for i in range(depth):
        kind, j = i % N_MIXERS, i // N_MIXERS
        if kind == 0:
            w_uq = _pad_heads(mla_w_uq[j], MLA_HEADS, MLA_NOPE + MLA_ROPE, HEAD_PAD).astype(BF16)
            cq, ckv, kr = _mla_down(h, mla_w_down, j, mla_q_norm[j], mla_kv_norm[j], cos_t, sin_t)
            q = _q_up(cq, w_uq, cos_t, sin_t, bsz, seq)
            k, vt = _kv_up(ckv, mla_w_ukv[j].astype(BF16), kr, bsz, seq)
            a = _attention(q, k, vt)
            w_o = mla_w_o
        elif kind == 1:
            a = _conv_in(h, conv_w_in, j, conv_w[j], seq)
            w_o = conv_w_out
        else:
            d_inner = ssm_w_out.shape[1]
            nh = ssm_dt_bias.shape[1]
            n_main = ssm_w_in.shape[2] - nh
            z = _mm_w(h, ssm_w_in, j, epilogue=_identity, ncols=d_inner)
            xbc = _ssd_xbc(h, ssm_w_in, j, ssm_conv_w[j], ssm_conv_b[j], d_inner, seq)
            dt = _ssd_dt(h, ssm_w_in[j, :, n_main:].astype(BF16), ssm_dt_bias[j])
            hg = nh // SSM_GROUPS
            gw = d_inner // SSM_GROUPS
            dt_g = dt.reshape(t, SSM_GROUPS, hg).transpose(1, 0, 2)
            per_lane = lambda p: jnp.repeat(p.reshape(SSM_GROUPS, hg), SSM_HEAD_DIM, axis=1).reshape(SSM_GROUPS, 1, gw)
            a = _ssd_scan(z, xbc, dt_g, per_lane(ssm_a_log[j]), per_lane(ssm_d[j]), ssm_norm[j], bsz, seq)
            w_o = ssm_w_out
        xf, h = _mm_resid(a, w_o.astype(BF16), j, xf, mods[i, 0], norm_post[i, 0], (mods[i, 1], norm_pre[i, 1]), seq)
        u, w_down16 = _mm_w(h, mlp_up, i, epilogue=_relu2, cast_stack=mlp_down)
        nxt = (mods[i + 1, 0], norm_pre[i + 1, 0]) if i + 1 < depth else None
        xf, h = _mm_resid(u, w_down16[None], 0, xf, mods[i, 1], norm_post[i, 1], nxt, seq)
    return xf.reshape(bsz, seq, d)
```

```python
import functools
import math

import jax
import jax.numpy as jnp
from jax import lax
from jax.experimental import pallas as pl
from jax.experimental.pallas import tpu as pltpu

F32 = jnp.float32
BF16 = jnp.bfloat16

EPS = 1e-6
CHUNK = 64
N_MIXERS = 3
MLA_HEADS = 16
MLA_LORA = 512
MLA_NOPE = 128
MLA_ROPE = 64
MLA_V = 128
ROPE_THETA = 10000.0
SSM_HEAD_DIM = 64
SSM_GROUPS = 8
SSM_STATE = 128

LANES = 128
SUBLANES = 8
V7X_VMEM_BYTES = 64 * 1024 * 1024
HEAD_PAD = 2 * LANES


def _vmem_limit(block_bytes, scratch_bytes=0, temp_bytes=0):
    need = 2 * sum(block_bytes) + scratch_bytes + temp_bytes + (4 << 20)
    return int(min(need, V7X_VMEM_BYTES - (6 << 20)))


def _nbytes(shape, dtype):
    return math.prod(shape) * jnp.dtype(dtype).itemsize


def _params(sem, limit):
    return pltpu.CompilerParams(dimension_semantics=sem, vmem_limit_bytes=limit)


def _rms(x):
    return x * lax.rsqrt(jnp.mean(x * x, axis=-1, keepdims=True) + EPS)


def _modulated_norm(x, mod_ref, g_ref):
    h = _rms(x) * g_ref[...]
    return h * (1.0 + mod_ref[1:2, :]) + mod_ref[0:1, :]


def _silu(x):
    return x * (1.0 / (1.0 + jnp.exp(-x)))


def _ada_kernel(c_ref, w_ref, b_ref, o_ref):
    c_act = _silu(c_ref[...]).astype(BF16)
    y = jnp.dot(c_act, w_ref[...].astype(BF16), preferred_element_type=F32)
    o_ref[...] = y + b_ref[...]


def _ada_all(c, ada_w, ada_b):
    nsub = ada_w.shape[0] * ada_w.shape[1]
    bsz, d = c.shape
    n = ada_w.shape[-1]
    w = ada_w.reshape(nsub, d, n)
    b = ada_b.reshape(nsub, 1, n)
    tn = 1024
    blocks = [_nbytes((d, tn), F32), _nbytes((bsz, tn), F32)]
    return pl.pallas_call(
        _ada_kernel,
        out_shape=jax.ShapeDtypeStruct((nsub, bsz, n), F32),
        grid=(nsub, n // tn),
        in_specs=[
            pl.BlockSpec((bsz, d), lambda s, j: (0, 0)),
            pl.BlockSpec((None, d, tn), lambda s, j: (s, 0, j)),
            pl.BlockSpec((None, 1, tn), lambda s, j: (s, 0, j)),
        ],
        out_specs=pl.BlockSpec((None, bsz, tn), lambda s, j: (s, 0, j)),
        compiler_params=_params(("arbitrary", "arbitrary"), _vmem_limit(blocks, temp_bytes=_nbytes((d, tn), BF16))),
        name="ada_mod",
    )(c, w, b)


def _rope_kernel(pos_ref, freq_ref, cos_ref, sin_ref):
    ang = pos_ref[...] * freq_ref[...]
    lane = lax.broadcasted_iota(jnp.int32, ang.shape, 1)
    half = MLA_ROPE // 2
    cos_ref[...] = jnp.where(lane < MLA_ROPE, jnp.cos(ang), 0.0)
    s = jnp.sin(ang)
    sin_ref[...] = jnp.where(lane < half, -s, jnp.where(lane < MLA_ROPE, s, 0.0))


def _rope_tables(positions):
    t = positions.size
    half = MLA_ROPE // 2
    inv_freq = ROPE_THETA ** (-jnp.arange(0, MLA_ROPE, 2, dtype=F32) / MLA_ROPE)
    freq = jnp.concatenate([inv_freq, inv_freq, jnp.zeros((LANES - 2 * half,), F32)]).reshape(1, LANES)
    pos = positions.astype(F32).reshape(t, 1)
    tm = min(t, 1024)
    out = jax.ShapeDtypeStruct((t, LANES), F32)
    return pl.pallas_call(
        _rope_kernel,
        out_shape=(out, out),
        grid=(t // tm,),
        in_specs=[pl.BlockSpec((tm, 1), lambda i: (i, 0)), pl.BlockSpec((1, LANES), lambda i: (0, 0))],
        out_specs=(pl.BlockSpec((tm, LANES), lambda i: (i, 0)), pl.BlockSpec((tm, LANES), lambda i: (i, 0))),
        compiler_params=_params(("arbitrary",), _vmem_limit([_nbytes((tm, LANES), F32)] * 3, temp_bytes=8 << 20)),
        name="rope_tables",
    )(pos, freq)


def _rope_slab(z, cos_ref, sin_ref):
    half = MLA_ROPE // 2
    lane = lax.broadcasted_iota(jnp.int32, z.shape, 1)
    partner = jnp.where(lane < half, pltpu.roll(z, LANES - half, axis=1), pltpu.roll(z, half, axis=1))
    return z * cos_ref[...] + partner * sin_ref[...]


def _relu2(y):
    r = jnp.maximum(y, 0.0)
    return r * r


def _norm_mod_kernel(x_ref, mod_ref, g_ref, h_ref):
    h_ref[...] = _modulated_norm(x_ref[...], mod_ref, g_ref).astype(BF16)


def _norm_mod(x, mod, g, seq, *, tm=1024):
    t, d = x.shape
    tiles_per_batch = seq // tm
    blocks = [_nbytes((tm, d), F32), _nbytes((tm, d), BF16)]
    return pl.pallas_call(
        _norm_mod_kernel,
        out_shape=jax.ShapeDtypeStruct((t, d), BF16),
        grid=(t // tm,),
        in_specs=[
            pl.BlockSpec((tm, d), lambda i: (i, 0)),
            pl.BlockSpec((None, 3, d), lambda i: (i // tiles_per_batch, 0, 0)),
            pl.BlockSpec((1, d), lambda i: (0, 0)),
        ],
        out_specs=pl.BlockSpec((tm, d), lambda i: (i, 0)),
        compiler_params=_params(("arbitrary",), _vmem_limit(blocks, temp_bytes=2 * _nbytes((tm, d), F32))),
        name="norm_mod",
    )(x, mod, g.reshape(1, d))


def _mm_w_kernel(*refs, epilogue, side_cast):
    if side_cast:
        h_ref, w_ref, c_ref, o_ref, c16_ref, w_scr = refs
        c16_ref[...] = c_ref[...].astype(BF16)
    else:
        h_ref, w_ref, o_ref, w_scr = refs

    @pl.when(pl.program_id(1) == 0)
    def _():
        w_scr[...] = w_ref[...].astype(BF16)

    y = jnp.dot(h_ref[...], w_scr[...], preferred_element_type=F32)
    o_ref[...] = epilogue(y).astype(o_ref.dtype)


def _identity(y):
    return y


def _mm_w(h, w_stack, layer, *, epilogue, ncols=None, cast_stack=None, tm=1024, tn=1024):
    t, d = h.shape
    n = w_stack.shape[2] if ncols is None else ncols
    nj, ni = n // tn, t // tm
    blocks = [_nbytes((tm, d), BF16), _nbytes((d, tn), F32), _nbytes((tm, tn), BF16)]
    in_specs = [
        pl.BlockSpec((tm, d), lambda j, i: (i, 0)),
        pl.BlockSpec((None, d, tn), lambda j, i: (layer, 0, j)),
    ]
    args = [h, w_stack]
    out_shape = [jax.ShapeDtypeStruct((t, n), BF16)]
    out_specs = [pl.BlockSpec((tm, tn), lambda j, i: (i, j))]
    if cast_stack is not None:
        rows, cols = cast_stack.shape[1:]
        assert rows % (nj * ni * 2 * SUBLANES) == 0, "side-cast slabs must tile the weight rows exactly"
        slab = rows // (nj * ni)
        in_specs.append(pl.BlockSpec((None, slab, cols), lambda j, i: (layer, j * ni + i, 0)))
        args.append(cast_stack)
        out_shape.append(jax.ShapeDtypeStruct((rows, cols), BF16))
        out_specs.append(pl.BlockSpec((slab, cols), lambda j, i: (j * ni + i, 0)))
        blocks += [_nbytes((slab, cols), F32), _nbytes((slab, cols), BF16)]
    outs = pl.pallas_call(
        functools.partial(_mm_w_kernel, epilogue=epilogue, side_cast=cast_stack is not None),
        out_shape=tuple(out_shape),
        grid=(nj, ni),
        in_specs=in_specs,
        out_specs=tuple(out_specs),
        scratch_shapes=[pltpu.VMEM((d, tn), BF16)],
        compiler_params=_params(("arbitrary", "arbitrary"),
                                _vmem_limit(blocks, _nbytes((d, tn), BF16), 2 * _nbytes((tm, tn), F32))),
        name="mm_w",
    )(*args)
    return outs if cast_stack is not None else outs[0]


_EPILOGUE_ROWS = 256


def _mm_resid_kernel(*refs, nk, emit_next):
    a_ref, w_ref, x_ref, mod_ref, g_ref = refs[:5]
    refs = refs[5:]
    if emit_next:
        modn_ref, gn_ref, o_ref, hn_ref = refs[:4]
        refs = refs[4:]
    else:
        o_ref = refs[0]
        refs = refs[1:]
    i, k = pl.program_id(0), pl.program_id(1)
    tm = o_ref.shape[0]

    if nk > 1:
        x_scr, x_sem = refs

        def x_copy():
            return pltpu.make_async_copy(x_ref.at[pl.ds(pl.multiple_of(i * tm, tm), tm), :], x_scr, x_sem)

        def partial_product(rows):
            return jnp.dot(a_ref[rows, :], w_ref[...], preferred_element_type=F32)

        @pl.when(k == 0)
        def _():
            x_copy().start()
            for r0 in range(0, tm, _EPILOGUE_ROWS):
                rows = slice(r0, r0 + _EPILOGUE_ROWS)
                o_ref[rows, :] = partial_product(rows)

        if nk > 2:
            @pl.when((k > 0) & (k < nk - 1))
            def _():
                for r0 in range(0, tm, _EPILOGUE_ROWS):
                    rows = slice(r0, r0 + _EPILOGUE_ROWS)
                    o_ref[rows, :] += partial_product(rows)
    else:
        x_scr = x_ref

    def finish():
        gate_g = mod_ref[2:3, :] * g_ref[...]
        if emit_next:
            scale_g = gn_ref[...] * (1.0 + modn_ref[1:2, :])
        for r0 in range(0, tm, _EPILOGUE_ROWS):
            rows = slice(r0, r0 + _EPILOGUE_ROWS)
            y = jnp.dot(a_ref[rows, :], w_ref[...], preferred_element_type=F32)
            if nk > 1:
                y = o_ref[rows, :] + y
            x_new = x_scr[rows, :] + _rms(y) * gate_g
            o_ref[rows, :] = x_new
            if emit_next:
                hn_ref[rows, :] = (_rms(x_new) * scale_g + modn_ref[0:1, :]).astype(BF16)

    if nk > 1:
        @pl.when(k == nk - 1)
        def _():
            x_copy().wait()
            finish()
    else:
        finish()


def _mm_resid(a, w_stack, layer, x, mod, g, nxt, seq):
    t, kdim = a.shape
    d = w_stack.shape[2]
    tm, tk = 512, min(kdim, 2048)
    nk = kdim // tk
    tiles_per_batch = seq // tm
    emit_next = nxt is not None
    batch_map = lambda i, k: (i // tiles_per_batch, 0, 0)
    fixed = lambda i, k: (0, 0)
    x_spec = pl.BlockSpec((tm, d), lambda i, k: (i, 0)) if nk == 1 else pl.BlockSpec(memory_space=pl.ANY)
    in_specs = [
        pl.BlockSpec((tm, tk), lambda i, k: (i, k)),
        pl.BlockSpec((None, tk, d), lambda i, k: (layer, k, 0)),
        x_spec,
        pl.BlockSpec((None, 3, d), batch_map),
        pl.BlockSpec((1, d), fixed),
    ]
    args = [a, w_stack, x, mod, g.reshape(1, d)]
    out_shape = [jax.ShapeDtypeStruct((t, d), F32)]
    out_specs = [pl.BlockSpec((tm, d), lambda i, k: (i, 0))]
    blocks = [_nbytes((tm, tk), BF16), _nbytes((tk, d), BF16), _nbytes((tm, d), F32)]
    if emit_next:
        in_specs += [pl.BlockSpec((None, 3, d), batch_map), pl.BlockSpec((1, d), fixed)]
        args += [nxt[0], nxt[1].reshape(1, d)]
        out_shape.append(jax.ShapeDtypeStruct((t, d), BF16))
        out_specs.append(pl.BlockSpec((tm, d), lambda i, k: (i, 0)))
        blocks.append(_nbytes((tm, d), BF16))
    if nk == 1:
        scratch, scratch_bytes = [], 0
        blocks.append(_nbytes((tm, d), F32))
    else:
        scratch, scratch_bytes = [pltpu.VMEM((tm, d), F32), pltpu.SemaphoreType.DMA(())], _nbytes((tm, d), F32)
    outs = pl.pallas_call(
        functools.partial(_mm_resid_kernel, nk=nk, emit_next=emit_next),
        out_shape=tuple(out_shape),
        grid=(t // tm, nk),
        in_specs=in_specs,
        out_specs=tuple(out_specs),
        scratch_shapes=scratch,
        compiler_params=_params(("arbitrary", "arbitrary"),
                                _vmem_limit(blocks, scratch_bytes, 8 * _nbytes((_EPILOGUE_ROWS, d), F32))),
        name="mm_resid",
    )(*args)
    return (outs[0], outs[1]) if emit_next else (outs[0], None)


def _mla_down_kernel(h_ref, w_ref, qn_ref, kvn_ref, cos_ref, sin_ref, cq_ref, ckv_ref, kr_ref, w_scr):
    @pl.when(pl.program_id(0) == 0)
    def _():
        w_scr[...] = w_ref[...].astype(BF16)

    y = jnp.dot(h_ref[...], w_scr[...], preferred_element_type=F32)
    cq_ref[...] = (_rms(y[:, :MLA_LORA]) * qn_ref[...]).astype(BF16)
    ckv_ref[...] = (_rms(y[:, MLA_LORA:2 * MLA_LORA]) * kvn_ref[...]).astype(BF16)
    slab = jnp.concatenate([y[:, 2 * MLA_LORA:], jnp.zeros((y.shape[0], LANES - MLA_ROPE), F32)], axis=1)
    kr_ref[...] = _rope_slab(slab, cos_ref, sin_ref).astype(BF16)


def _mla_down(h, w_stack, layer, q_norm, kv_norm, cos_t, sin_t, *, tm=512):
    t, d = h.shape
    n = w_stack.shape[2]
    blocks = [_nbytes((tm, d), BF16), _nbytes((d, n), F32), 3 * _nbytes((tm, MLA_LORA), BF16), 2 * _nbytes((tm, LANES), F32)]
    row = lambda i: (i, 0)
    fixed = lambda i: (0, 0)
    return pl.pallas_call(
        _mla_down_kernel,
        out_shape=(jax.ShapeDtypeStruct((t, MLA_LORA), BF16), jax.ShapeDtypeStruct((t, MLA_LORA), BF16),
                   jax.ShapeDtypeStruct((t, LANES), BF16)),
        grid=(t // tm,),
        in_specs=[
            pl.BlockSpec((tm, d), row),
            pl.BlockSpec((None, d, n), lambda i: (layer, 0, 0)),
            pl.BlockSpec((1, MLA_LORA), fixed),
            pl.BlockSpec((1, MLA_LORA), fixed),
            pl.BlockSpec((tm, LANES), row),
            pl.BlockSpec((tm, LANES), row),
        ],
        out_specs=(pl.BlockSpec((tm, MLA_LORA), row), pl.BlockSpec((tm, MLA_LORA), row), pl.BlockSpec((tm, LANES), row)),
        scratch_shapes=[pltpu.VMEM((d, n), BF16)],
        compiler_params=_params(("arbitrary",), _vmem_limit(blocks, _nbytes((d, n), BF16), 3 * _nbytes((tm, n), F32))),
        name="mla_down",
    )(h, w_stack, q_norm.reshape(1, -1), kv_norm.reshape(1, -1), cos_t, sin_t)


def _q_up_kernel(a_ref, w_ref, cos_ref, sin_ref, q_ref, *, heads, scale):
    y = jnp.dot(a_ref[...], w_ref[...], preferred_element_type=F32) * scale
    for h in range(heads):
        base = h * HEAD_PAD
        q_ref[h, :, :LANES] = y[:, base:base + LANES].astype(BF16)
        q_ref[h, :, LANES:] = _rope_slab(y[:, base + LANES:base + HEAD_PAD], cos_ref, sin_ref).astype(BF16)


def _kv_up_kernel(a_ref, w_ref, kr_ref, k_ref, vt_ref, *, heads):
    y = jnp.dot(a_ref[...], w_ref[...], preferred_element_type=F32)
    kr = kr_ref[...]
    for h in range(heads):
        base = h * (MLA_NOPE + MLA_V)
        k_ref[h, :, :LANES] = y[:, base:base + MLA_NOPE].astype(BF16)
        k_ref[h, :, LANES:] = kr
        vt_ref[h] = y[:, base + MLA_NOPE:base + MLA_NOPE + MLA_V].T.astype(BF16)


def _q_up(cq, w_uq, cos_t, sin_t, bsz, seq, *, tm=1024, heads=4):
    t, r = cq.shape
    tiles_per_batch = seq // tm
    scale = math.log2(math.e) / math.sqrt(MLA_NOPE + MLA_ROPE)
    tn = heads * HEAD_PAD
    blocks = [_nbytes((tm, r), BF16), _nbytes((r, tn), BF16), 2 * _nbytes((tm, LANES), F32), _nbytes((heads, tm, HEAD_PAD), BF16)]
    return pl.pallas_call(
        functools.partial(_q_up_kernel, heads=heads, scale=scale),
        out_shape=jax.ShapeDtypeStruct((bsz, MLA_HEADS, seq, HEAD_PAD), BF16),
        grid=(t // tm, MLA_HEADS // heads),
        in_specs=[
            pl.BlockSpec((tm, r), lambda i, j: (i, 0)),
            pl.BlockSpec((r, tn), lambda i, j: (0, j)),
            pl.BlockSpec((tm, LANES), lambda i, j: (i, 0)),
            pl.BlockSpec((tm, LANES), lambda i, j: (i, 0)),
        ],
        out_specs=pl.BlockSpec((None, heads, tm, HEAD_PAD),
                               lambda i, j: (i // tiles_per_batch, j, i % tiles_per_batch, 0)),
        compiler_params=_params(("arbitrary", "arbitrary"), _vmem_limit(blocks, temp_bytes=2 * _nbytes((tm, tn), F32))),
        name="mla_q_up",
    )(cq, w_uq, cos_t, sin_t)


def _kv_up(ckv, w_ukv, kr, bsz, seq, *, tm=1024, heads=4):
    t, r = ckv.shape
    tiles_per_batch = seq // tm
    tn = heads * (MLA_NOPE + MLA_V)
    blocks = [_nbytes((tm, r), BF16), _nbytes((r, tn), BF16), _nbytes((tm, LANES), BF16),
              _nbytes((heads, tm, HEAD_PAD), BF16), _nbytes((heads, tm, MLA_V), BF16)]
    out_map = lambda i, j: (i // tiles_per_batch, j, i % tiles_per_batch, 0)
    vt_map = lambda i, j: (i // tiles_per_batch, j, 0, i % tiles_per_batch)
    return pl.pallas_call(
        functools.partial(_kv_up_kernel, heads=heads),
        out_shape=(jax.ShapeDtypeStruct((bsz, MLA_HEADS, seq, HEAD_PAD), BF16),
                   jax.ShapeDtypeStruct((bsz, MLA_HEADS, MLA_V, seq), BF16)),
        grid=(t // tm, MLA_HEADS // heads),
        in_specs=[
            pl.BlockSpec((tm, r), lambda i, j: (i, 0)),
            pl.BlockSpec((r, tn), lambda i, j: (0, j)),
            pl.BlockSpec((tm, LANES), lambda i, j: (i, 0)),
        ],
        out_specs=(pl.BlockSpec((None, heads, tm, HEAD_PAD), out_map), pl.BlockSpec((None, heads, MLA_V, tm), vt_map)),
        compiler_params=_params(("arbitrary", "arbitrary"), _vmem_limit(blocks, temp_bytes=2 * _nbytes((tm, tn), F32))),
        name="mla_kv_up",
    )(ckv, w_ukv, kr)


def _attn_kernel(q_ref, k_ref, vt_ref, o_ref, *scratch, tq, tk, lanes_q):
    def tile(qi, carry):
        _attn_tile(qi, q_ref, k_ref, vt_ref, o_ref, *scratch, tq=tq, tk=tk, lanes_q=lanes_q)
        return carry

    lax.fori_loop(0, q_ref.shape[0] // tq, tile, 0)


def _attn_tile(qi, q_ref, k_ref, vt_ref, o_ref, m_scr, l_scr, acc_scr, sa_scr, sb_scr, bma_scr, bmb_scr, *, tq, tk, lanes_q):
    ngrp = tq // lanes_q

    def q_rows(c):
        return pl.ds(pl.multiple_of(qi * tq + c * lanes_q, lanes_q), lanes_q)

    ndiag = tq // tk
    assert ndiag == 2, "the two-slot score ring below assumes two key blocks per query tile"
    nfull = qi * ndiag
    ring = ((sa_scr, bma_scr), (sb_scr, bmb_scr))
    m_scr[...] = jnp.full(m_scr.shape, -jnp.inf, F32)
    l_scr[...] = jnp.zeros(l_scr.shape, F32)
    acc_scr[...] = jnp.zeros(acc_scr.shape, F32)

    def skipped(c, diag):
        return diag is not None and (c + 1) * lanes_q <= diag * tk

    def scores(blk, diag, slot):
        s_scr, bm_scr = ring[slot]
        kb = k_ref[pl.ds(pl.multiple_of(blk * tk, tk), tk), :]
        for c in range(ngrp):
            if skipped(c, diag):
                continue
            qc = q_ref[q_rows(c), :]
            s = lax.dot_general(kb, qc, (((1,), (1,)), ((), ())), preferred_element_type=F32)
            if diag is not None and c * lanes_q < (diag + 1) * tk:
                k_chunk = (lax.broadcasted_iota(jnp.int32, s.shape, 0) + diag * tk) // CHUNK
                q_chunk = (lax.broadcasted_iota(jnp.int32, s.shape, 1) + c * lanes_q) // CHUNK
                s = jnp.where(k_chunk <= q_chunk, s, -jnp.inf)
            s_scr[c] = s
            bm_scr[c] = jnp.max(s, axis=0, keepdims=True)

    def accumulate(blk, diag, slot):
        s_scr, bm_scr = ring[slot]
        vtb = vt_ref[:, pl.ds(pl.multiple_of(blk * tk, tk), tk)]
        for c in range(ngrp):
            if skipped(c, diag):
                continue
            m_prev = m_scr[c]
            m_new = jnp.maximum(m_prev, bm_scr[c])
            alpha = jnp.exp2(m_prev - m_new)
            p = jnp.exp2(s_scr[c] - m_new)
            l_scr[c] = alpha * l_scr[c] + jnp.sum(p, axis=0, keepdims=True)
            acc_scr[c] = alpha * acc_scr[c] + jnp.dot(vtb, p.astype(BF16), preferred_element_type=F32)
            m_scr[c] = m_new

    @pl.when(qi > 0)
    def _():
        scores(0, None, 0)

    @pl.when(qi == 0)
    def _():
        scores(0, 0, 0)

    def body(i, carry):
        b0 = 2 * i
        scores(b0 + 1, None, 1)
        accumulate(b0, None, 0)
        scores(b0 + 2, None, 0)
        accumulate(b0 + 1, None, 1)
        return carry

    lax.fori_loop(0, qi - 1, body, 0)

    @pl.when(qi > 0)
    def _():
        scores(nfull - 1, None, 1)
        accumulate(nfull - 2, None, 0)
        scores(nfull, 0, 0)
        accumulate(nfull - 1, None, 1)

    scores(nfull + 1, 1, 1)
    accumulate(nfull, 0, 0)
    accumulate(nfull + 1, 1, 1)
    for c in range(ngrp):
        o_ref[q_rows(c), :] = (acc_scr[c] / l_scr[c]).T.astype(o_ref.dtype)


def _attention(q, k, vt, *, tq=1024, tk=512, lanes_q=256):
    bsz, heads, seq, _ = q.shape
    ngrp = tq // lanes_q
    blocks = [_nbytes((seq, HEAD_PAD), BF16), _nbytes((seq, HEAD_PAD), BF16), _nbytes((MLA_V, seq), BF16), _nbytes((seq, MLA_V), BF16)]
    scratch = (2 * _nbytes((ngrp, SUBLANES, lanes_q), F32) + _nbytes((ngrp, MLA_V, lanes_q), F32)
               + 2 * _nbytes((ngrp, tk, lanes_q), F32) + 2 * _nbytes((ngrp, SUBLANES, lanes_q), F32))
    return pl.pallas_call(
        functools.partial(_attn_kernel, tq=tq, tk=tk, lanes_q=lanes_q),
        out_shape=jax.ShapeDtypeStruct((bsz * seq, heads * MLA_V), BF16),
        grid=(bsz, heads),
        in_specs=[
            pl.BlockSpec((None, None, seq, HEAD_PAD), lambda b, h: (b, h, 0, 0)),
            pl.BlockSpec((None, None, seq, HEAD_PAD), lambda b, h: (b, h, 0, 0)),
            pl.BlockSpec((None, None, MLA_V, seq), lambda b, h: (b, h, 0, 0)),
        ],
        out_specs=pl.BlockSpec((seq, MLA_V), lambda b, h: (b, h)),
        scratch_shapes=[pltpu.VMEM((ngrp, 1, lanes_q), F32), pltpu.VMEM((ngrp, 1, lanes_q), F32),
                        pltpu.VMEM((ngrp, MLA_V, lanes_q), F32),
                        pltpu.VMEM((ngrp, tk, lanes_q), F32), pltpu.VMEM((ngrp, tk, lanes_q), F32),
                        pltpu.VMEM((ngrp, 1, lanes_q), F32), pltpu.VMEM((ngrp, 1, lanes_q), F32)],
        compiler_params=_params(("arbitrary", "arbitrary"),
                                _vmem_limit(blocks, scratch, 8 * _nbytes((tk, tq), F32))),
        name="mla_attention",
    )(q, k, vt)


def _causal_conv(t, tail_ref, w_ref, first_of_batch):
    kw = w_ref.shape[0]
    tm = t.shape[0]
    hist = jnp.where(first_of_batch, 0.0, tail_ref[...])
    tail_ref[...] = t[tm - SUBLANES:, :]
    row = lax.broadcasted_iota(jnp.int32, hist.shape, 0)
    acc = t * w_ref[kw - 1:kw, :]
    for k in range(kw - 1):
        s = kw - 1 - k
        rolled = pltpu.roll(t, s, axis=0)
        head = jnp.where(row < s, pltpu.roll(hist, s, axis=0), rolled[:SUBLANES])
        acc = acc + jnp.concatenate([head, rolled[SUBLANES:]], axis=0) * w_ref[k:k + 1, :]
    return acc


def _conv_in_kernel(h_ref, wb_ref, wc_ref, wu_ref, cw_ref, o_ref, wb_scr, wc_scr, wu_scr, tail_scr, *, tiles_per_batch):
    i = pl.program_id(1)

    @pl.when(i == 0)
    def _():
        wb_scr[...] = wb_ref[...].astype(BF16)
        wc_scr[...] = wc_ref[...].astype(BF16)
        wu_scr[...] = wu_ref[...].astype(BF16)

    h = h_ref[...]
    gc = jnp.dot(h, wc_scr[...], preferred_element_type=F32)
    u = jnp.dot(h, wu_scr[...], preferred_element_type=F32)
    conv = _causal_conv(gc * u, tail_scr, cw_ref, i % tiles_per_batch == 0)
    gb = jnp.dot(h, wb_scr[...], preferred_element_type=F32)
    o_ref[...] = (gb * conv).astype(o_ref.dtype)


def _conv_in(h, w_stack, layer, conv_w, seq, *, tm=512, tn=512):
    t, d = h.shape
    n = w_stack.shape[2] // 3
    nj = n // tn
    tiles_per_batch = seq // tm
    kw = conv_w.shape[0]
    blocks = [_nbytes((tm, d), BF16), 3 * _nbytes((d, tn), F32), _nbytes((tm, tn), BF16)]
    scratch = 3 * _nbytes((d, tn), BF16) + _nbytes((SUBLANES, tn), F32)
    wspec = lambda off: pl.BlockSpec((None, d, tn), lambda j, i: (layer, 0, j + off * nj))
    return pl.pallas_call(
        functools.partial(_conv_in_kernel, tiles_per_batch=tiles_per_batch),
        out_shape=jax.ShapeDtypeStruct((t, n), BF16),
        grid=(nj, t // tm),
        in_specs=[
            pl.BlockSpec((tm, d), lambda j, i: (i, 0)),
            wspec(0), wspec(1), wspec(2),
            pl.BlockSpec((kw, tn), lambda j, i: (0, j)),
        ],
        out_specs=pl.BlockSpec((tm, tn), lambda j, i: (i, j)),
        scratch_shapes=[pltpu.VMEM((d, tn), BF16), pltpu.VMEM((d, tn), BF16), pltpu.VMEM((d, tn), BF16),
                        pltpu.VMEM((SUBLANES, tn), F32)],
        compiler_params=_params(("arbitrary", "arbitrary"), _vmem_limit(blocks, scratch, 6 * _nbytes((tm, tn), F32))),
        name="conv_in",
    )(h, w_stack, w_stack, w_stack, conv_w)


def _softplus(x):
    return jnp.maximum(x, 0.0) + jnp.log1p(jnp.exp(-jnp.abs(x)))


def _ssd_xbc_kernel(h_ref, w_ref, cw_ref, cb_ref, o_ref, w_scr, tail_scr, y_scr, *, tiles_per_batch):
    i = pl.program_id(1)

    @pl.when(i == 0)
    def _():
        w_scr[...] = w_ref[...].astype(BF16)
        y_scr[...] = jnp.zeros(y_scr.shape, F32)
        tail_scr[...] = jnp.zeros(tail_scr.shape, F32)

    conv = _causal_conv(y_scr[...], tail_scr, cw_ref, (i - 1) % tiles_per_batch == 0) + cb_ref[...]
    o_ref[...] = _silu(conv).astype(o_ref.dtype)
    y_scr[...] = jnp.dot(h_ref[...], w_scr[...], preferred_element_type=F32)


def _ssd_xbc(h, w_stack, layer, conv_w, conv_b, col0, seq, *, tm=512, tn=1024):
    t, d = h.shape
    kw, n = conv_w.shape
    ni = t // tm
    tiles_per_batch = seq // tm
    joff = col0 // tn
    blocks = [_nbytes((tm, d), BF16), _nbytes((d, tn), F32), _nbytes((tm, tn), BF16)]
    scratch = _nbytes((d, tn), BF16) + _nbytes((SUBLANES, tn), F32) + _nbytes((tm, tn), F32)
    return pl.pallas_call(
        functools.partial(_ssd_xbc_kernel, tiles_per_batch=tiles_per_batch),
        out_shape=jax.ShapeDtypeStruct((t, n), BF16),
        grid=(n // tn, ni + 1),
        in_specs=[
            pl.BlockSpec((tm, d), lambda j, i: (jnp.minimum(i, ni - 1), 0)),
            pl.BlockSpec((None, d, tn), lambda j, i: (layer, 0, j + joff)),
            pl.BlockSpec((kw, tn), lambda j, i: (0, j)),
            pl.BlockSpec((1, tn), lambda j, i: (0, j)),
        ],
        out_specs=pl.BlockSpec((tm, tn), lambda j, i: (jnp.maximum(i - 1, 0), j)),
        scratch_shapes=[pltpu.VMEM((d, tn), BF16), pltpu.VMEM((SUBLANES, tn), F32), pltpu.VMEM((tm, tn), F32)],
        compiler_params=_params(("arbitrary", "arbitrary"), _vmem_limit(blocks, scratch, 6 * _nbytes((tm, tn), F32))),
        name="ssd_xbc",
    )(h, w_stack, conv_w, conv_b.reshape(1, -1))


def _ssd_dt_kernel(h_ref, w_ref, b_ref, dt_ref):
    dt_ref[...] = _softplus(jnp.dot(h_ref[...], w_ref[...], preferred_element_type=F32) + b_ref[...])


def _ssd_dt(h, w_dt, dt_bias, *, tm=1024):
    t, d = h.shape
    nh = w_dt.shape[1]
    blocks = [_nbytes((tm, d), BF16), _nbytes((d, LANES), BF16), _nbytes((tm, LANES), F32)]
    return pl.pallas_call(
        _ssd_dt_kernel,
        out_shape=jax.ShapeDtypeStruct((t, nh), F32),
        grid=(t // tm,),
        in_specs=[
            pl.BlockSpec((tm, d), lambda i: (i, 0)),
            pl.BlockSpec((d, nh), lambda i: (0, 0)),
            pl.BlockSpec((1, nh), lambda i: (0, 0)),
        ],
        out_specs=pl.BlockSpec((tm, nh), lambda i: (i, 0)),
        compiler_params=_params(("arbitrary",), _vmem_limit(blocks, temp_bytes=4 * _nbytes((tm, LANES), F32))),
        name="ssd_dt",
    )(h, w_dt, dt_bias.reshape(1, nh))


def _split3(x):
    hi = x.astype(BF16)
    r1 = x - hi.astype(F32)
    mid = r1.astype(BF16)
    lo = (r1 - mid.astype(F32)).astype(BF16)
    return hi, mid, lo


def _dot01(sel, x, *, sel_left):
    out = None
    for part in _split3(x):
        d = (jnp.dot(sel, part, preferred_element_type=F32) if sel_left
             else jnp.dot(part, sel, preferred_element_type=F32))
        out = d if out is None else out + d
    return out


def _ssd_kernel(z_ref, x_ref, b_ref, c_ref, dt_ref, alog_ref, dskip_ref, nw_ref, o_ref, state_scr, *, tt):
    hg = dt_ref.shape[-1]
    gw = x_ref.shape[-1]
    hd = gw // hg
    nch = tt // CHUNK

    @pl.when(pl.program_id(2) == 0)
    def _():
        state_scr[...] = jnp.zeros(state_scr.shape, F32)

    def iota(shape, axis):
        return lax.broadcasted_iota(jnp.int32, shape, axis)

    expand = (iota((hg, gw), 1) // hd == iota((hg, gw), 0)).astype(BF16)
    r2, c2 = iota((tt, tt), 0), iota((tt, tt), 1)
    tri = ((c2 <= r2) & (c2 // CHUNK == r2 // CHUNK)).astype(BF16)
    lrow, lcol = iota((CHUNK, gw), 0), iota((CHUNK, gw), 1) % hd
    half = gw // 2
    bd_keep = iota((half * CHUNK // hd, half), 0) // CHUNK == iota((half * CHUNK // hd, half), 1) // hd

    a_head = -jnp.exp(alog_ref[...]) * math.log2(math.e)
    dt_head = dt_ref[...]
    acum_head = _dot01(tri, dt_head * a_head, sel_left=True)
    dt_all = _dot01(expand, dt_head, sel_left=False)
    acum_all = _dot01(expand, acum_head, sel_left=False)

    for ci in range(nch):
        rows = slice(ci * CHUNK, (ci + 1) * CHUNK)
        x = x_ref[rows, :].astype(F32)
        bm = b_ref[rows, :]
        cm = c_ref[rows, :]
        dt, acum = dt_all[rows], acum_all[rows]
        r = jnp.sum(jnp.where(lrow == lcol, acum, 0.0), axis=0, keepdims=True)
        decay = jnp.exp2(jnp.where(lrow >= lcol, acum - r, -jnp.inf))
        cb = lax.dot_general(cm, bm, (((1,), (1,)), ((), ())), preferred_element_type=F32)
        m = (jnp.concatenate([cb] * hg, axis=1) * decay).astype(BF16)
        xdt = x * dt
        xdt16 = xdt.astype(BF16)
        ydiag = []
        for hf in range(2):
            cols = slice(hf * half, (hf + 1) * half)
            blockdiag = jnp.where(bd_keep, jnp.concatenate([xdt16[:, cols]] * (half // hd), axis=0), jnp.zeros((), BF16))
            ydiag.append(jnp.dot(m[:, cols], blockdiag, preferred_element_type=F32))
        last = acum[CHUNK - 1:CHUNK, :]
        xw = (xdt * jnp.exp2(last - acum)).astype(BF16)
        st = lax.dot_general(bm, xw, (((0,), (0,)), ((), ())), preferred_element_type=F32)
        prev = state_scr[...]
        yoff = jnp.dot(cm, prev.astype(BF16), preferred_element_type=F32) * jnp.exp2(acum)
        state_scr[...] = prev * jnp.exp2(last) + st
        y = jnp.concatenate(ydiag, axis=1) + yoff + x * dskip_ref[...]
        gated = y * _silu(z_ref[rows, :].astype(F32))
        o_ref[rows, :] = (_rms(gated) * nw_ref[...]).astype(o_ref.dtype)


def _ssd_scan(z, xbc, dt_g, a_log, d_skip, norm_w, bsz, seq, *, tt=256):
    t, d_inner = z.shape
    groups, _, hg = dt_g.shape
    gw = d_inner // groups
    n = SSM_STATE
    nt = seq // tt
    boff = d_inner // n
    coff = boff + groups
    row = lambda b, g, i: b * nt + i
    blocks = [2 * _nbytes((tt, gw), BF16), 2 * _nbytes((tt, n), BF16), _nbytes((tt, LANES), F32), _nbytes((tt, gw), BF16)]
    return pl.pallas_call(
        functools.partial(_ssd_kernel, tt=tt),
        out_shape=jax.ShapeDtypeStruct((t, d_inner), BF16),
        grid=(bsz, groups, nt),
        in_specs=[
            pl.BlockSpec((tt, gw), lambda b, g, i: (row(b, g, i), g)),
            pl.BlockSpec((tt, gw), lambda b, g, i: (row(b, g, i), g)),
            pl.BlockSpec((tt, n), lambda b, g, i: (row(b, g, i), boff + g)),
            pl.BlockSpec((tt, n), lambda b, g, i: (row(b, g, i), coff + g)),
            pl.BlockSpec((None, tt, hg), lambda b, g, i: (g, row(b, g, i), 0)),
            pl.BlockSpec((None, 1, hg), lambda b, g, i: (g, 0, 0)),
            pl.BlockSpec((None, 1, gw), lambda b, g, i: (g, 0, 0)),
            pl.BlockSpec((1, gw), lambda b, g, i: (0, g)),
        ],
        out_specs=pl.BlockSpec((tt, gw), lambda b, g, i: (row(b, g, i), g)),
        scratch_shapes=[pltpu.VMEM((n, gw), F32)],
        compiler_params=_params(("arbitrary", "arbitrary", "arbitrary"),
                                _vmem_limit(blocks, _nbytes((n, gw), F32), 24 * _nbytes((tt, gw), F32))),
        name="ssd_scan",
    )(z, xbc, xbc, xbc, dt_g, a_log, d_skip, norm_w.reshape(1, d_inner))


def _pad_heads(w, heads, width, padded):
    r = w.shape[0]
    w = w.reshape(r, heads, width)
    return jnp.pad(w, ((0, 0), (0, 0), (0, padded - width))).reshape(r, heads * padded)


def kernel(x, c, positions, ada_w, ada_b, norm_pre, norm_post, mla_w_down, mla_q_norm, mla_w_uq, mla_kv_norm, mla_w_ukv, mla_w_o, conv_w_in, conv_w, conv_w_out, ssm_w_in, ssm_conv_w, ssm_conv_b, ssm_dt_bias, ssm_a_log, ssm_d, ssm_norm, ssm_w_out, mlp_up, mlp_down):
    bsz, seq, d = x.shape
    depth = ada_w.shape[0]
    t = bsz * seq
    xf = x.reshape(t, d)

    mods = _ada_all(c, ada_w, ada_b).reshape(depth, 2, bsz, 3, d)
    cos_t, sin_t = _rope_tables(positions)

    h = _norm_mod(xf, mods[0, 0], norm_pre[0, 0], seq)
    for i in range(depth):
        kind, j = i % N_MIXERS, i // N_MIXERS
        if kind == 0:
            w_uq = _pad_heads(mla_w_uq[j], MLA_HEADS, MLA_NOPE + MLA_ROPE, HEAD_PAD).astype(BF16)
            cq, ckv, kr = _mla_down(h, mla_w_down, j, mla_q_norm[j], mla_kv_norm[j], cos_t, sin_t)
            q = _q_up(cq, w_uq, cos_t, sin_t, bsz, seq)
            k, vt = _kv_up(ckv, mla_w_ukv[j].astype(BF16), kr, bsz, seq)
            a = _attention(q, k, vt)
            w_o = mla_w_o
        elif kind == 1:
            a = _conv_in(h, conv_w_in, j, conv_w[j], seq)
            w_o = conv_w_out
        else:
            d_inner = ssm_w_out.shape[1]
            nh = ssm_dt_bias.shape[1]
            n_main = ssm_w_in.shape[2] - nh
            z = _mm_w(h, ssm_w_in, j, epilogue=_identity, ncols=d_inner)
            xbc = _ssd_xbc(h, ssm_w_in, j, ssm_conv_w[j], ssm_conv_b[j], d_inner, seq)
            dt = _ssd_dt(h, ssm_w_in[j, :, n_main:].astype(BF16), ssm_dt_bias[j])
            hg = nh // SSM_GROUPS
            gw = d_inner // SSM_GROUPS
            dt_g = dt.reshape(t, SSM_GROUPS, hg).transpose(1, 0, 2)
            per_lane = lambda p: jnp.repeat(p.reshape(SSM_GROUPS, hg), SSM_HEAD_DIM, axis=1).reshape(SSM_GROUPS, 1, gw)
            a = _ssd_scan(z, xbc, dt_g, ssm_a_log[j].reshape(SSM_GROUPS, 1, hg), per_lane(ssm_d[j]), ssm_norm[j], bsz, seq)
            w_o = ssm_w_out
        xf, h = _mm_resid(a, w_o.astype(BF16), j, xf, mods[i, 0], norm_post[i, 0], (mods[i, 1], norm_pre[i, 1]), seq)
        u, w_down16 = _mm_w(h, mlp_up, i, epilogue=_relu2, cast_stack=mlp_down)
        nxt = (mods[i + 1, 0], norm_pre[i + 1, 0]) if i + 1 < depth else None
        xf, h = _mm_resid(u, w_down16[None], 0, xf, mods[i, 1], norm_post[i, 1], nxt, seq)
    return xf.reshape(bsz, seq, d)
```

```python
import functools
import math

import jax
import jax.numpy as jnp
from jax import lax
from jax.experimental import pallas as pl
from jax.experimental.pallas import tpu as pltpu

F32 = jnp.float32
BF16 = jnp.bfloat16

EPS = 1e-6
CHUNK = 64
N_MIXERS = 3
MLA_HEADS = 16
MLA_LORA = 512
MLA_NOPE = 128
MLA_ROPE = 64
MLA_V = 128
ROPE_THETA = 10000.0
SSM_HEAD_DIM = 64
SSM_GROUPS = 8
SSM_STATE = 128

LANES = 128
SUBLANES = 8
V7X_VMEM_BYTES = 64 * 1024 * 1024
HEAD_PAD = 2 * LANES


def _vmem_limit(block_bytes, scratch_bytes=0, temp_bytes=0):
    need = 2 * sum(block_bytes) + scratch_bytes + temp_bytes + (4 << 20)
    return int(min(need, V7X_VMEM_BYTES - (6 << 20)))


def _nbytes(shape, dtype):
    return math.prod(shape) * jnp.dtype(dtype).itemsize


def _params(sem, limit):
    return pltpu.CompilerParams(dimension_semantics=sem, vmem_limit_bytes=limit)


def _rms(x):
    return x * lax.rsqrt(jnp.mean(x * x, axis=-1, keepdims=True) + EPS)


def _modulated_norm(x, mod_ref, g_ref):
    h = _rms(x) * g_ref[...]
    return h * (1.0 + mod_ref[1:2, :]) + mod_ref[0:1, :]


def _silu(x):
    return x * (1.0 / (1.0 + jnp.exp(-x)))


def _ada_kernel(c_ref, w_ref, b_ref, o_ref):
    c_act = _silu(c_ref[...]).astype(BF16)
    y = jnp.dot(c_act, w_ref[...].astype(BF16), preferred_element_type=F32)
    o_ref[...] = y + b_ref[...]


def _ada_all(c, ada_w, ada_b):
    nsub = ada_w.shape[0] * ada_w.shape[1]
    bsz, d = c.shape
    n = ada_w.shape[-1]
    w = ada_w.reshape(nsub, d, n)
    b = ada_b.reshape(nsub, 1, n)
    tn = 1024
    blocks = [_nbytes((d, tn), F32), _nbytes((bsz, tn), F32)]
    return pl.pallas_call(
        _ada_kernel,
        out_shape=jax.ShapeDtypeStruct((nsub, bsz, n), F32),
        grid=(nsub, n // tn),
        in_specs=[
            pl.BlockSpec((bsz, d), lambda s, j: (0, 0)),
            pl.BlockSpec((None, d, tn), lambda s, j: (s, 0, j)),
            pl.BlockSpec((None, 1, tn), lambda s, j: (s, 0, j)),
        ],
        out_specs=pl.BlockSpec((None, bsz, tn), lambda s, j: (s, 0, j)),
        compiler_params=_params(("arbitrary", "arbitrary"), _vmem_limit(blocks, temp_bytes=_nbytes((d, tn), BF16))),
        name="ada_mod",
    )(c, w, b)


def _rope_kernel(pos_ref, freq_ref, cos_ref, sin_ref):
    ang = pos_ref[...] * freq_ref[...]
    lane = lax.broadcasted_iota(jnp.int32, ang.shape, 1)
    half = MLA_ROPE // 2
    cos_ref[...] = jnp.where(lane < MLA_ROPE, jnp.cos(ang), 0.0)
    s = jnp.sin(ang)
    sin_ref[...] = jnp.where(lane < half, -s, jnp.where(lane < MLA_ROPE, s, 0.0))


def _rope_tables(positions):
    t = positions.size
    half = MLA_ROPE // 2
    inv_freq = ROPE_THETA ** (-jnp.arange(0, MLA_ROPE, 2, dtype=F32) / MLA_ROPE)
    freq = jnp.concatenate([inv_freq, inv_freq, jnp.zeros((LANES - 2 * half,), F32)]).reshape(1, LANES)
    pos = positions.astype(F32).reshape(t, 1)
    tm = min(t, 1024)
    out = jax.ShapeDtypeStruct((t, LANES), F32)
    return pl.pallas_call(
        _rope_kernel,
        out_shape=(out, out),
        grid=(t // tm,),
        in_specs=[pl.BlockSpec((tm, 1), lambda i: (i, 0)), pl.BlockSpec((1, LANES), lambda i: (0, 0))],
        out_specs=(pl.BlockSpec((tm, LANES), lambda i: (i, 0)), pl.BlockSpec((tm, LANES), lambda i: (i, 0))),
        compiler_params=_params(("arbitrary",), _vmem_limit([_nbytes((tm, LANES), F32)] * 3, temp_bytes=8 << 20)),
        name="rope_tables",
    )(pos, freq)


def _rope_slab(z, cos_ref, sin_ref):
    half = MLA_ROPE // 2
    lane = lax.broadcasted_iota(jnp.int32, z.shape, 1)
    partner = jnp.where(lane < half, pltpu.roll(z, LANES - half, axis=1), pltpu.roll(z, half, axis=1))
    return z * cos_ref[...] + partner * sin_ref[...]


def _relu2(y):
    r = jnp.maximum(y, 0.0)
    return r * r


def _norm_mod_kernel(x_ref, mod_ref, g_ref, h_ref):
    h_ref[...] = _modulated_norm(x_ref[...], mod_ref, g_ref).astype(BF16)


def _norm_mod(x, mod, g, seq, *, tm=1024):
    t, d = x.shape
    tiles_per_batch = seq // tm
    blocks = [_nbytes((tm, d), F32), _nbytes((tm, d), BF16)]
    return pl.pallas_call(
        _norm_mod_kernel,
        out_shape=jax.ShapeDtypeStruct((t, d), BF16),
        grid=(t // tm,),
        in_specs=[
            pl.BlockSpec((tm, d), lambda i: (i, 0)),
            pl.BlockSpec((None, 3, d), lambda i: (i // tiles_per_batch, 0, 0)),
            pl.BlockSpec((1, d), lambda i: (0, 0)),
        ],
        out_specs=pl.BlockSpec((tm, d), lambda i: (i, 0)),
        compiler_params=_params(("arbitrary",), _vmem_limit(blocks, temp_bytes=2 * _nbytes((tm, d), F32))),
        name="norm_mod",
    )(x, mod, g.reshape(1, d))


def _mm_w_kernel(*refs, epilogue, side_cast):
    if side_cast:
        h_ref, w_ref, c_ref, o_ref, c16_ref, w_scr = refs
        c16_ref[...] = c_ref[...].astype(BF16)
    else:
        h_ref, w_ref, o_ref, w_scr = refs

    @pl.when(pl.program_id(1) == 0)
    def _():
        w_scr[...] = w_ref[...].astype(BF16)

    y = jnp.dot(h_ref[...], w_scr[...], preferred_element_type=F32)
    o_ref[...] = epilogue(y).astype(o_ref.dtype)


def _identity(y):
    return y


def _mm_w(h, w_stack, layer, *, epilogue, ncols=None, cast_stack=None, tm=1024, tn=1024):
    t, d = h.shape
    n = w_stack.shape[2] if ncols is None else ncols
    nj, ni = n // tn, t // tm
    blocks = [_nbytes((tm, d), BF16), _nbytes((d, tn), F32), _nbytes((tm, tn), BF16)]
    in_specs = [
        pl.BlockSpec((tm, d), lambda j, i: (i, 0)),
        pl.BlockSpec((None, d, tn), lambda j, i: (layer, 0, j)),
    ]
    args = [h, w_stack]
    out_shape = [jax.ShapeDtypeStruct((t, n), BF16)]
    out_specs = [pl.BlockSpec((tm, tn), lambda j, i: (i, j))]
    if cast_stack is not None:
        rows, cols = cast_stack.shape[1:]
        assert rows % (nj * ni * 2 * SUBLANES) == 0, "side-cast slabs must tile the weight rows exactly"
        slab = rows // (nj * ni)
        in_specs.append(pl.BlockSpec((None, slab, cols), lambda j, i: (layer, j * ni + i, 0)))
        args.append(cast_stack)
        out_shape.append(jax.ShapeDtypeStruct((rows, cols), BF16))
        out_specs.append(pl.BlockSpec((slab, cols), lambda j, i: (j * ni + i, 0)))
        blocks += [_nbytes((slab, cols), F32), _nbytes((slab, cols), BF16)]
    outs = pl.pallas_call(
        functools.partial(_mm_w_kernel, epilogue=epilogue, side_cast=cast_stack is not None),
        out_shape=tuple(out_shape),
        grid=(nj, ni),
        in_specs=in_specs,
        out_specs=tuple(out_specs),
        scratch_shapes=[pltpu.VMEM((d, tn), BF16)],
        compiler_params=_params(("arbitrary", "arbitrary"),
                                _vmem_limit(blocks, _nbytes((d, tn), BF16), 2 * _nbytes((tm, tn), F32))),
        name="mm_w",
    )(*args)
    return outs if cast_stack is not None else outs[0]


_EPILOGUE_ROWS = 256


def _mm_resid_kernel(*refs, nk, emit_next):
    a_ref, w_ref, x_ref, mod_ref, g_ref = refs[:5]
    refs = refs[5:]
    if emit_next:
        modn_ref, gn_ref, o_ref, hn_ref = refs[:4]
        refs = refs[4:]
    else:
        o_ref = refs[0]
        refs = refs[1:]
    i, k = pl.program_id(0), pl.program_id(1)
    tm = o_ref.shape[0]

    if nk > 1:
        x_scr, x_sem = refs

        def x_copy():
            return pltpu.make_async_copy(x_ref.at[pl.ds(pl.multiple_of(i * tm, tm), tm), :], x_scr, x_sem)

        def partial_product(rows):
            return jnp.dot(a_ref[rows, :], w_ref[...], preferred_element_type=F32)

        @pl.when(k == 0)
        def _():
            x_copy().start()
            for r0 in range(0, tm, _EPILOGUE_ROWS):
                rows = slice(r0, r0 + _EPILOGUE_ROWS)
                o_ref[rows, :] = partial_product(rows)

        if nk > 2:
            @pl.when((k > 0) & (k < nk - 1))
            def _():
                for r0 in range(0, tm, _EPILOGUE_ROWS):
                    rows = slice(r0, r0 + _EPILOGUE_ROWS)
                    o_ref[rows, :] += partial_product(rows)
    else:
        x_scr = x_ref

    def finish():
        gate_g = mod_ref[2:3, :] * g_ref[...]
        if emit_next:
            scale_g = gn_ref[...] * (1.0 + modn_ref[1:2, :])
        for r0 in range(0, tm, _EPILOGUE_ROWS):
            rows = slice(r0, r0 + _EPILOGUE_ROWS)
            y = jnp.dot(a_ref[rows, :], w_ref[...], preferred_element_type=F32)
            if nk > 1:
                y = o_ref[rows, :] + y
            x_new = x_scr[rows, :] + _rms(y) * gate_g
            o_ref[rows, :] = x_new
            if emit_next:
                hn_ref[rows, :] = (_rms(x_new) * scale_g + modn_ref[0:1, :]).astype(BF16)

    if nk > 1:
        @pl.when(k == nk - 1)
        def _():
            x_copy().wait()
            finish()
    else:
        finish()


def _mm_resid(a, w_stack, layer, x, mod, g, nxt, seq):
    t, kdim = a.shape
    d = w_stack.shape[2]
    tm, tk = 512, min(kdim, 2048)
    nk = kdim // tk
    tiles_per_batch = seq // tm
    emit_next = nxt is not None
    batch_map = lambda i, k: (i // tiles_per_batch, 0, 0)
    fixed = lambda i, k: (0, 0)
    x_spec = pl.BlockSpec((tm, d), lambda i, k: (i, 0)) if nk == 1 else pl.BlockSpec(memory_space=pl.ANY)
    in_specs = [
        pl.BlockSpec((tm, tk), lambda i, k: (i, k)),
        pl.BlockSpec((None, tk, d), lambda i, k: (layer, k, 0)),
        x_spec,
        pl.BlockSpec((None, 3, d), batch_map),
        pl.BlockSpec((1, d), fixed),
    ]
    args = [a, w_stack, x, mod, g.reshape(1, d)]
    out_shape = [jax.ShapeDtypeStruct((t, d), F32)]
    out_specs = [pl.BlockSpec((tm, d), lambda i, k: (i, 0))]
    blocks = [_nbytes((tm, tk), BF16), _nbytes((tk, d), BF16), _nbytes((tm, d), F32)]
    if emit_next:
        in_specs += [pl.BlockSpec((None, 3, d), batch_map), pl.BlockSpec((1, d), fixed)]
        args += [nxt[0], nxt[1].reshape(1, d)]
        out_shape.append(jax.ShapeDtypeStruct((t, d), BF16))
        out_specs.append(pl.BlockSpec((tm, d), lambda i, k: (i, 0)))
        blocks.append(_nbytes((tm, d), BF16))
    if nk == 1:
        scratch, scratch_bytes = [], 0
        blocks.append(_nbytes((tm, d), F32))
    else:
        scratch, scratch_bytes = [pltpu.VMEM((tm, d), F32), pltpu.SemaphoreType.DMA(())], _nbytes((tm, d), F32)
    outs = pl.pallas_call(
        functools.partial(_mm_resid_kernel, nk=nk, emit_next=emit_next),
        out_shape=tuple(out_shape),
        grid=(t // tm, nk),
        in_specs=in_specs,
        out_specs=tuple(out_specs),
        scratch_shapes=scratch,
        compiler_params=_params(("arbitrary", "arbitrary"),
                                _vmem_limit(blocks, scratch_bytes, 8 * _nbytes((_EPILOGUE_ROWS, d), F32))),
        name="mm_resid",
    )(*args)
    return (outs[0], outs[1]) if emit_next else (outs[0], None)


def _mla_down_kernel(h_ref, w_ref, qn_ref, kvn_ref, cos_ref, sin_ref, cq_ref, ckv_ref, kr_ref, w_scr):
    @pl.when(pl.program_id(0) == 0)
    def _():
        w_scr[...] = w_ref[...].astype(BF16)

    y = jnp.dot(h_ref[...], w_scr[...], preferred_element_type=F32)
    cq_ref[...] = (_rms(y[:, :MLA_LORA]) * qn_ref[...]).astype(BF16)
    ckv_ref[...] = (_rms(y[:, MLA_LORA:2 * MLA_LORA]) * kvn_ref[...]).astype(BF16)
    slab = jnp.concatenate([y[:, 2 * MLA_LORA:], jnp.zeros((y.shape[0], LANES - MLA_ROPE), F32)], axis=1)
    kr_ref[...] = _rope_slab(slab, cos_ref, sin_ref).astype(BF16)


def _mla_down(h, w_stack, layer, q_norm, kv_norm, cos_t, sin_t, *, tm=512):
    t, d = h.shape
    n = w_stack.shape[2]
    blocks = [_nbytes((tm, d), BF16), _nbytes((d, n), F32), 3 * _nbytes((tm, MLA_LORA), BF16), 2 * _nbytes((tm, LANES), F32)]
    row = lambda i: (i, 0)
    fixed = lambda i: (0, 0)
    return pl.pallas_call(
        _mla_down_kernel,
        out_shape=(jax.ShapeDtypeStruct((t, MLA_LORA), BF16), jax.ShapeDtypeStruct((t, MLA_LORA), BF16),
                   jax.ShapeDtypeStruct((t, LANES), BF16)),
        grid=(t // tm,),
        in_specs=[
            pl.BlockSpec((tm, d), row),
            pl.BlockSpec((None, d, n), lambda i: (layer, 0, 0)),
            pl.BlockSpec((1, MLA_LORA), fixed),
            pl.BlockSpec((1, MLA_LORA), fixed),
            pl.BlockSpec((tm, LANES), row),
            pl.BlockSpec((tm, LANES), row),
        ],
        out_specs=(pl.BlockSpec((tm, MLA_LORA), row), pl.BlockSpec((tm, MLA_LORA), row), pl.BlockSpec((tm, LANES), row)),
        scratch_shapes=[pltpu.VMEM((d, n), BF16)],
        compiler_params=_params(("arbitrary",), _vmem_limit(blocks, _nbytes((d, n), BF16), 3 * _nbytes((tm, n), F32))),
        name="mla_down",
    )(h, w_stack, q_norm.reshape(1, -1), kv_norm.reshape(1, -1), cos_t, sin_t)


def _q_up_kernel(a_ref, w_ref, cos_ref, sin_ref, q_ref, *, heads, scale):
    y = jnp.dot(a_ref[...], w_ref[...], preferred_element_type=F32) * scale
    for h in range(heads):
        base = h * HEAD_PAD
        q_ref[h, :, :LANES] = y[:, base:base + LANES].astype(BF16)
        q_ref[h, :, LANES:] = _rope_slab(y[:, base + LANES:base + HEAD_PAD], cos_ref, sin_ref).astype(BF16)


def _kv_up_kernel(a_ref, w_ref, kr_ref, k_ref, vt_ref, *, heads):
    y = jnp.dot(a_ref[...], w_ref[...], preferred_element_type=F32)
    kr = kr_ref[...]
    for h in range(heads):
        base = h * (MLA_NOPE + MLA_V)
        k_ref[h, :, :LANES] = y[:, base:base + MLA_NOPE].astype(BF16)
        k_ref[h, :, LANES:] = kr
        vt_ref[h] = y[:, base + MLA_NOPE:base + MLA_NOPE + MLA_V].T.astype(BF16)


def _q_up(cq, w_uq, cos_t, sin_t, bsz, seq, *, tm=1024, heads=8):
    t, r = cq.shape
    tiles_per_batch = seq // tm
    scale = math.log2(math.e) / math.sqrt(MLA_NOPE + MLA_ROPE)
    tn = heads * HEAD_PAD
    blocks = [_nbytes((tm, r), BF16), _nbytes((r, tn), BF16), 2 * _nbytes((tm, LANES), F32), _nbytes((heads, tm, HEAD_PAD), BF16)]
    return pl.pallas_call(
        functools.partial(_q_up_kernel, heads=heads, scale=scale),
        out_shape=jax.ShapeDtypeStruct((bsz, MLA_HEADS, seq, HEAD_PAD), BF16),
        grid=(t // tm, MLA_HEADS // heads),
        in_specs=[
            pl.BlockSpec((tm, r), lambda i, j: (i, 0)),
            pl.BlockSpec((r, tn), lambda i, j: (0, j)),
            pl.BlockSpec((tm, LANES), lambda i, j: (i, 0)),
            pl.BlockSpec((tm, LANES), lambda i, j: (i, 0)),
        ],
        out_specs=pl.BlockSpec((None, heads, tm, HEAD_PAD),
                               lambda i, j: (i // tiles_per_batch, j, i % tiles_per_batch, 0)),
        compiler_params=_params(("arbitrary", "arbitrary"), _vmem_limit(blocks, temp_bytes=2 * _nbytes((tm, tn), F32))),
        name="mla_q_up",
    )(cq, w_uq, cos_t, sin_t)


def _kv_up(ckv, w_ukv, kr, bsz, seq, *, tm=1024, heads=8):
    t, r = ckv.shape
    tiles_per_batch = seq // tm
    tn = heads * (MLA_NOPE + MLA_V)
    blocks = [_nbytes((tm, r), BF16), _nbytes((r, tn), BF16), _nbytes((tm, LANES), BF16),
              _nbytes((heads, tm, HEAD_PAD), BF16), _nbytes((heads, tm, MLA_V), BF16)]
    out_map = lambda i, j: (i // tiles_per_batch, j, i % tiles_per_batch, 0)
    vt_map = lambda i, j: (i // tiles_per_batch, j, 0, i % tiles_per_batch)
    return pl.pallas_call(
        functools.partial(_kv_up_kernel, heads=heads),
        out_shape=(jax.ShapeDtypeStruct((bsz, MLA_HEADS, seq, HEAD_PAD), BF16),
                   jax.ShapeDtypeStruct((bsz, MLA_HEADS, MLA_V, seq), BF16)),
        grid=(t // tm, MLA_HEADS // heads),
        in_specs=[
            pl.BlockSpec((tm, r), lambda i, j: (i, 0)),
            pl.BlockSpec((r, tn), lambda i, j: (0, j)),
            pl.BlockSpec((tm, LANES), lambda i, j: (i, 0)),
        ],
        out_specs=(pl.BlockSpec((None, heads, tm, HEAD_PAD), out_map), pl.BlockSpec((None, heads, MLA_V, tm), vt_map)),
        compiler_params=_params(("arbitrary", "arbitrary"), _vmem_limit(blocks, temp_bytes=2 * _nbytes((tm, tn), F32))),
        name="mla_kv_up",
    )(ckv, w_ukv, kr)


def _attn_kernel(q_ref, k_ref, vt_ref, o_ref, *scratch, tq, tk, lanes_q):
    def tile(qi, carry):
        _attn_tile(qi, q_ref, k_ref, vt_ref, o_ref, *scratch, tq=tq, tk=tk, lanes_q=lanes_q)
        return carry

    lax.fori_loop(0, q_ref.shape[0] // tq, tile, 0)


def _attn_tile(qi, q_ref, k_ref, vt_ref, o_ref, m_scr, l_scr, acc_scr, sa_scr, sb_scr, bma_scr, bmb_scr, *, tq, tk, lanes_q):
    ngrp = tq // lanes_q

    def q_rows(c):
        return pl.ds(pl.multiple_of(qi * tq + c * lanes_q, lanes_q), lanes_q)

    ndiag = tq // tk
    assert ndiag == 2, "the two-slot score ring below assumes two key blocks per query tile"
    nfull = qi * ndiag
    ring = ((sa_scr, bma_scr), (sb_scr, bmb_scr))
    m_scr[...] = jnp.full(m_scr.shape, -jnp.inf, F32)
    l_scr[...] = jnp.zeros(l_scr.shape, F32)
    acc_scr[...] = jnp.zeros(acc_scr.shape, F32)

    def skipped(c, diag):
        return diag is not None and (c + 1) * lanes_q <= diag * tk

    def scores(blk, diag, slot):
        s_scr, bm_scr = ring[slot]
        kb = k_ref[pl.ds(pl.multiple_of(blk * tk, tk), tk), :]
        for c in range(ngrp):
            if skipped(c, diag):
                continue
            qc = q_ref[q_rows(c), :]
            s = lax.dot_general(kb, qc, (((1,), (1,)), ((), ())), preferred_element_type=F32)
            if diag is not None and c * lanes_q < (diag + 1) * tk:
                k_chunk = (lax.broadcasted_iota(jnp.int32, s.shape, 0) + diag * tk) // CHUNK
                q_chunk = (lax.broadcasted_iota(jnp.int32, s.shape, 1) + c * lanes_q) // CHUNK
                s = jnp.where(k_chunk <= q_chunk, s, -jnp.inf)
            s_scr[c] = s
            bm_scr[c] = jnp.max(s, axis=0, keepdims=True)

    def accumulate(blk, diag, slot):
        s_scr, bm_scr = ring[slot]
        vtb = vt_ref[:, pl.ds(pl.multiple_of(blk * tk, tk), tk)]
        for c in range(ngrp):
            if skipped(c, diag):
                continue
            m_prev = m_scr[c]
            m_new = jnp.maximum(m_prev, bm_scr[c])
            alpha = jnp.exp2(m_prev - m_new)
            p = jnp.exp2(s_scr[c] - m_new)
            l_scr[c] = alpha * l_scr[c] + jnp.sum(p, axis=0, keepdims=True)
            acc_scr[c] = alpha * acc_scr[c] + jnp.dot(vtb, p.astype(BF16), preferred_element_type=F32)
            m_scr[c] = m_new

    @pl.when(qi > 0)
    def _():
        scores(0, None, 0)

    @pl.when(qi == 0)
    def _():
        scores(0, 0, 0)

    def body(i, carry):
        b0 = 2 * i
        scores(b0 + 1, None, 1)
        accumulate(b0, None, 0)
        scores(b0 + 2, None, 0)
        accumulate(b0 + 1, None, 1)
        return carry

    lax.fori_loop(0, qi - 1, body, 0)

    @pl.when(qi > 0)
    def _():
        scores(nfull - 1, None, 1)
        accumulate(nfull - 2, None, 0)
        scores(nfull, 0, 0)
        accumulate(nfull - 1, None, 1)

    scores(nfull + 1, 1, 1)
    accumulate(nfull, 0, 0)
    accumulate(nfull + 1, 1, 1)
    for c in range(ngrp):
        o_ref[q_rows(c), :] = (acc_scr[c] / l_scr[c]).T.astype(o_ref.dtype)


def _attention(q, k, vt, *, tq=1024, tk=512, lanes_q=256):
    bsz, heads, seq, _ = q.shape
    ngrp = tq // lanes_q
    blocks = [_nbytes((seq, HEAD_PAD), BF16), _nbytes((seq, HEAD_PAD), BF16), _nbytes((MLA_V, seq), BF16), _nbytes((seq, MLA_V), BF16)]
    scratch = (2 * _nbytes((ngrp, SUBLANES, lanes_q), F32) + _nbytes((ngrp, MLA_V, lanes_q), F32)
               + 2 * _nbytes((ngrp, tk, lanes_q), F32) + 2 * _nbytes((ngrp, SUBLANES, lanes_q), F32))
    return pl.pallas_call(
        functools.partial(_attn_kernel, tq=tq, tk=tk, lanes_q=lanes_q),
        out_shape=jax.ShapeDtypeStruct((bsz * seq, heads * MLA_V), BF16),
        grid=(bsz, heads),
        in_specs=[
            pl.BlockSpec((None, None, seq, HEAD_PAD), lambda b, h: (b, h, 0, 0)),
            pl.BlockSpec((None, None, seq, HEAD_PAD), lambda b, h: (b, h, 0, 0)),
            pl.BlockSpec((None, None, MLA_V, seq), lambda b, h: (b, h, 0, 0)),
        ],
        out_specs=pl.BlockSpec((seq, MLA_V), lambda b, h: (b, h)),
        scratch_shapes=[pltpu.VMEM((ngrp, 1, lanes_q), F32), pltpu.VMEM((ngrp, 1, lanes_q), F32),
                        pltpu.VMEM((ngrp, MLA_V, lanes_q), F32),
                        pltpu.VMEM((ngrp, tk, lanes_q), F32), pltpu.VMEM((ngrp, tk, lanes_q), F32),
                        pltpu.VMEM((ngrp, 1, lanes_q), F32), pltpu.VMEM((ngrp, 1, lanes_q), F32)],
        compiler_params=_params(("arbitrary", "arbitrary"),
                                _vmem_limit(blocks, scratch, 8 * _nbytes((tk, tq), F32))),
        name="mla_attention",
    )(q, k, vt)


def _causal_conv(t, tail_ref, w_ref, first_of_batch):
    kw = w_ref.shape[0]
    tm = t.shape[0]
    hist = jnp.where(first_of_batch, 0.0, tail_ref[...])
    tail_ref[...] = t[tm - SUBLANES:, :]
    row = lax.broadcasted_iota(jnp.int32, hist.shape, 0)
    acc = t * w_ref[kw - 1:kw, :]
    for k in range(kw - 1):
        s = kw - 1 - k
        rolled = pltpu.roll(t, s, axis=0)
        head = jnp.where(row < s, pltpu.roll(hist, s, axis=0), rolled[:SUBLANES])
        acc = acc + jnp.concatenate([head, rolled[SUBLANES:]], axis=0) * w_ref[k:k + 1, :]
    return acc


def _conv_in_kernel(h_ref, wb_ref, wc_ref, wu_ref, cw_ref, o_ref, wb_scr, wc_scr, wu_scr, tail_scr, *, tiles_per_batch):
    i = pl.program_id(1)

    @pl.when(i == 0)
    def _():
        wb_scr[...] = wb_ref[...].astype(BF16)
        wc_scr[...] = wc_ref[...].astype(BF16)
        wu_scr[...] = wu_ref[...].astype(BF16)

    h = h_ref[...]
    gc = jnp.dot(h, wc_scr[...], preferred_element_type=F32)
    u = jnp.dot(h, wu_scr[...], preferred_element_type=F32)
    conv = _causal_conv(gc * u, tail_scr, cw_ref, i % tiles_per_batch == 0)
    gb = jnp.dot(h, wb_scr[...], preferred_element_type=F32)
    o_ref[...] = (gb * conv).astype(o_ref.dtype)


def _conv_in(h, w_stack, layer, conv_w, seq, *, tm=512, tn=512):
    t, d = h.shape
    n = w_stack.shape[2] // 3
    nj = n // tn
    tiles_per_batch = seq // tm
    kw = conv_w.shape[0]
    blocks = [_nbytes((tm, d), BF16), 3 * _nbytes((d, tn), F32), _nbytes((tm, tn), BF16)]
    scratch = 3 * _nbytes((d, tn), BF16) + _nbytes((SUBLANES, tn), F32)
    wspec = lambda off: pl.BlockSpec((None, d, tn), lambda j, i: (layer, 0, j + off * nj))
    return pl.pallas_call(
        functools.partial(_conv_in_kernel, tiles_per_batch=tiles_per_batch),
        out_shape=jax.ShapeDtypeStruct((t, n), BF16),
        grid=(nj, t // tm),
        in_specs=[
            pl.BlockSpec((tm, d), lambda j, i: (i, 0)),
            wspec(0), wspec(1), wspec(2),
            pl.BlockSpec((kw, tn), lambda j, i: (0, j)),
        ],
        out_specs=pl.BlockSpec((tm, tn), lambda j, i: (i, j)),
        scratch_shapes=[pltpu.VMEM((d, tn), BF16), pltpu.VMEM((d, tn), BF16), pltpu.VMEM((d, tn), BF16),
                        pltpu.VMEM((SUBLANES, tn), F32)],
        compiler_params=_params(("arbitrary", "arbitrary"), _vmem_limit(blocks, scratch, 6 * _nbytes((tm, tn), F32))),
        name="conv_in",
    )(h, w_stack, w_stack, w_stack, conv_w)


def _softplus(x):
    return jnp.maximum(x, 0.0) + jnp.log1p(jnp.exp(-jnp.abs(x)))


def _ssd_xbc_kernel(h_ref, w_ref, cw_ref, cb_ref, o_ref, w_scr, tail_scr, y_scr, *, tiles_per_batch):
    i = pl.program_id(1)

    @pl.when(i == 0)
    def _():
        w_scr[...] = w_ref[...].astype(BF16)
        y_scr[...] = jnp.zeros(y_scr.shape, F32)
        tail_scr[...] = jnp.zeros(tail_scr.shape, F32)

    conv = _causal_conv(y_scr[...], tail_scr, cw_ref, (i - 1) % tiles_per_batch == 0) + cb_ref[...]
    o_ref[...] = _silu(conv).astype(o_ref.dtype)
    y_scr[...] = jnp.dot(h_ref[...], w_scr[...], preferred_element_type=F32)


def _ssd_xbc(h, w_stack, layer, conv_w, conv_b, col0, seq, *, tm=512, tn=1024):
    t, d = h.shape
    kw, n = conv_w.shape
    ni = t // tm
    tiles_per_batch = seq // tm
    joff = col0 // tn
    blocks = [_nbytes((tm, d), BF16), _nbytes((d, tn), F32), _nbytes((tm, tn), BF16)]
    scratch = _nbytes((d, tn), BF16) + _nbytes((SUBLANES, tn), F32) + _nbytes((tm, tn), F32)
    return pl.pallas_call(
        functools.partial(_ssd_xbc_kernel, tiles_per_batch=tiles_per_batch),
        out_shape=jax.ShapeDtypeStruct((t, n), BF16),
        grid=(n // tn, ni + 1),
        in_specs=[
            pl.BlockSpec((tm, d), lambda j, i: (jnp.minimum(i, ni - 1), 0)),
            pl.BlockSpec((None, d, tn), lambda j, i: (layer, 0, j + joff)),
            pl.BlockSpec((kw, tn), lambda j, i: (0, j)),
            pl.BlockSpec((1, tn), lambda j, i: (0, j)),
        ],
        out_specs=pl.BlockSpec((tm, tn), lambda j, i: (jnp.maximum(i - 1, 0), j)),
        scratch_shapes=[pltpu.VMEM((d, tn), BF16), pltpu.VMEM((SUBLANES, tn), F32), pltpu.VMEM((tm, tn), F32)],
        compiler_params=_params(("arbitrary", "arbitrary"), _vmem_limit(blocks, scratch, 6 * _nbytes((tm, tn), F32))),
        name="ssd_xbc",
    )(h, w_stack, conv_w, conv_b.reshape(1, -1))


def _ssd_dt_kernel(h_ref, w_ref, b_ref, dt_ref):
    dt_ref[...] = _softplus(jnp.dot(h_ref[...], w_ref[...], preferred_element_type=F32) + b_ref[...])


def _ssd_dt(h, w_dt, dt_bias, *, tm=1024):
    t, d = h.shape
    nh = w_dt.shape[1]
    blocks = [_nbytes((tm, d), BF16), _nbytes((d, LANES), BF16), _nbytes((tm, LANES), F32)]
    return pl.pallas_call(
        _ssd_dt_kernel,
        out_shape=jax.ShapeDtypeStruct((t, nh), F32),
        grid=(t // tm,),
        in_specs=[
            pl.BlockSpec((tm, d), lambda i: (i, 0)),
            pl.BlockSpec((d, nh), lambda i: (0, 0)),
            pl.BlockSpec((1, nh), lambda i: (0, 0)),
        ],
        out_specs=pl.BlockSpec((tm, nh), lambda i: (i, 0)),
        compiler_params=_params(("arbitrary",), _vmem_limit(blocks, temp_bytes=4 * _nbytes((tm, LANES), F32))),
        name="ssd_dt",
    )(h, w_dt, dt_bias.reshape(1, nh))


def _split3(x):
    hi = x.astype(BF16)
    r1 = x - hi.astype(F32)
    mid = r1.astype(BF16)
    lo = (r1 - mid.astype(F32)).astype(BF16)
    return hi, mid, lo


def _dot01(sel, x, *, sel_left):
    out = None
    for part in _split3(x):
        d = (jnp.dot(sel, part, preferred_element_type=F32) if sel_left
             else jnp.dot(part, sel, preferred_element_type=F32))
        out = d if out is None else out + d
    return out


def _ssd_kernel(z_ref, x_ref, b_ref, c_ref, dt_ref, alog_ref, dskip_ref, nw_ref, o_ref, state_scr, *, tt):
    hg = dt_ref.shape[-1]
    gw = x_ref.shape[-1]
    hd = gw // hg
    nch = tt // CHUNK

    @pl.when(pl.program_id(2) == 0)
    def _():
        state_scr[...] = jnp.zeros(state_scr.shape, F32)

    def iota(shape, axis):
        return lax.broadcasted_iota(jnp.int32, shape, axis)

    expand = (iota((hg, gw), 1) // hd == iota((hg, gw), 0)).astype(BF16)
    r2, c2 = iota((tt, tt), 0), iota((tt, tt), 1)
    tri = ((c2 <= r2) & (c2 // CHUNK == r2 // CHUNK)).astype(BF16)
    lrow, lcol = iota((CHUNK, gw), 0), iota((CHUNK, gw), 1) % hd
    half = gw // 2
    bd_keep = iota((half * CHUNK // hd, half), 0) // CHUNK == iota((half * CHUNK // hd, half), 1) // hd

    a_head = -jnp.exp(alog_ref[...]) * math.log2(math.e)
    dt_head = dt_ref[...]
    acum_head = _dot01(tri, dt_head * a_head, sel_left=True)
    dt_all = _dot01(expand, dt_head, sel_left=False)
    acum_all = _dot01(expand, acum_head, sel_left=False)

    for ci in range(nch):
        rows = slice(ci * CHUNK, (ci + 1) * CHUNK)
        x = x_ref[rows, :].astype(F32)
        bm = b_ref[rows, :]
        cm = c_ref[rows, :]
        dt, acum = dt_all[rows], acum_all[rows]
        r = jnp.sum(jnp.where(lrow == lcol, acum, 0.0), axis=0, keepdims=True)
        decay = jnp.exp2(jnp.where(lrow >= lcol, acum - r, -jnp.inf))
        cb = lax.dot_general(cm, bm, (((1,), (1,)), ((), ())), preferred_element_type=F32)
        m = (jnp.concatenate([cb] * hg, axis=1) * decay).astype(BF16)
        xdt = x * dt
        xdt16 = xdt.astype(BF16)
        ydiag = []
        for hf in range(2):
            cols = slice(hf * half, (hf + 1) * half)
            blockdiag = jnp.where(bd_keep, jnp.concatenate([xdt16[:, cols]] * (half // hd), axis=0), jnp.zeros((), BF16))
            ydiag.append(jnp.dot(m[:, cols], blockdiag, preferred_element_type=F32))
        last = acum[CHUNK - 1:CHUNK, :]
        xw = (xdt * jnp.exp2(last - acum)).astype(BF16)
        st = lax.dot_general(bm, xw, (((0,), (0,)), ((), ())), preferred_element_type=F32)
        prev = state_scr[...]
        yoff = jnp.dot(cm, prev.astype(BF16), preferred_element_type=F32) * jnp.exp2(acum)
        state_scr[...] = prev * jnp.exp2(last) + st
        y = jnp.concatenate(ydiag, axis=1) + yoff + x * dskip_ref[...]
        gated = y * _silu(z_ref[rows, :].astype(F32))
        o_ref[rows, :] = (_rms(gated) * nw_ref[...]).astype(o_ref.dtype)


def _ssd_scan(z, xbc, dt_g, a_log, d_skip, norm_w, bsz, seq, *, tt=512):
    t, d_inner = z.shape
    groups, _, hg = dt_g.shape
    gw = d_inner // groups
    n = SSM_STATE
    nt = seq // tt
    boff = d_inner // n
    coff = boff + groups
    row = lambda b, g, i: b * nt + i
    blocks = [2 * _nbytes((tt, gw), BF16), 2 * _nbytes((tt, n), BF16), _nbytes((tt, LANES), F32), _nbytes((tt, gw), BF16)]
    return pl.pallas_call(
        functools.partial(_ssd_kernel, tt=tt),
        out_shape=jax.ShapeDtypeStruct((t, d_inner), BF16),
        grid=(bsz, groups, nt),
        in_specs=[
            pl.BlockSpec((tt, gw), lambda b, g, i: (row(b, g, i), g)),
            pl.BlockSpec((tt, gw), lambda b, g, i: (row(b, g, i), g)),
            pl.BlockSpec((tt, n), lambda b, g, i: (row(b, g, i), boff + g)),
            pl.BlockSpec((tt, n), lambda b, g, i: (row(b, g, i), coff + g)),
            pl.BlockSpec((None, tt, hg), lambda b, g, i: (g, row(b, g, i), 0)),
            pl.BlockSpec((None, 1, hg), lambda b, g, i: (g, 0, 0)),
            pl.BlockSpec((None, 1, gw), lambda b, g, i: (g, 0, 0)),
            pl.BlockSpec((1, gw), lambda b, g, i: (0, g)),
        ],
        out_specs=pl.BlockSpec((tt, gw), lambda b, g, i: (row(b, g, i), g)),
        scratch_shapes=[pltpu.VMEM((n, gw), F32)],
        compiler_params=_params(("arbitrary", "arbitrary", "arbitrary"),
                                _vmem_limit(blocks, _nbytes((n, gw), F32), 24 * _nbytes((tt, gw), F32))),
        name="ssd_scan",
    )(z, xbc, xbc, xbc, dt_g, a_log, d_skip, norm_w.reshape(1, d_inner))


def _pad_heads(w, heads, width, padded):
    r = w.shape[0]
    w = w.reshape(r, heads, width)
    return jnp.pad(w, ((0, 0), (0, 0), (0, padded - width))).reshape(r, heads * padded)


def kernel(x, c, positions, ada_w, ada_b, norm_pre, norm_post, mla_w_down, mla_q_norm, mla_w_uq, mla_kv_norm, mla_w_ukv, mla_w_o, conv_w_in, conv_w, conv_w_out, ssm_w_in, ssm_conv_w, ssm_conv_b, ssm_dt_bias, ssm_a_log, ssm_d, ssm_norm, ssm_w_out, mlp_up, mlp_down):
    bsz, seq, d = x.shape
    depth = ada_w.shape[0]
    t = bsz * seq
    xf = x.reshape(t, d)

    mods = _ada_all(c, ada_w, ada_b).reshape(depth, 2, bsz, 3, d)
    cos_t, sin_t = _rope_tables(positions)

    h = _norm_mod(xf, mods[0, 0], norm_pre[0, 0], seq)
    for i in range(depth):
        kind, j = i % N_MIXERS, i // N_MIXERS
        if kind == 0:
            w_uq = _pad_heads(mla_w_uq[j], MLA_HEADS, MLA_NOPE + MLA_ROPE, HEAD_PAD).astype(BF16)
            cq, ckv, kr = _mla_down(h, mla_w_down, j, mla_q_norm[j], mla_kv_norm[j], cos_t, sin_t)
            q = _q_up(cq, w_uq, cos_t, sin_t, bsz, seq)
            k, vt = _kv_up(ckv, mla_w_ukv[j].astype(BF16), kr, bsz, seq)
            a = _attention(q, k, vt)
            w_o = mla_w_o
        elif kind == 1:
            a = _conv_in(h, conv_w_in, j, conv_w[j], seq)
            w_o = conv_w_out
        else:
            d_inner = ssm_w_out.shape[1]
            nh = ssm_dt_bias.shape[1]
            n_main = ssm_w_in.shape[2] - nh
            z = _mm_w(h, ssm_w_in, j, epilogue=_identity, ncols=d_inner)
            xbc = _ssd_xbc(h, ssm_w_in, j, ssm_conv_w[j], ssm_conv_b[j], d_inner, seq)
            dt = _ssd_dt(h, ssm_w_in[j, :, n_main:].astype(BF16), ssm_dt_bias[j])
            hg = nh // SSM_GROUPS
            gw = d_inner // SSM_GROUPS
            dt_g = dt.reshape(t, SSM_GROUPS, hg).transpose(1, 0, 2)
            per_lane = lambda p: jnp.repeat(p.reshape(SSM_GROUPS, hg), SSM_HEAD_DIM, axis=1).reshape(SSM_GROUPS, 1, gw)
            a = _ssd_scan(z, xbc, dt_g, ssm_a_log[j].reshape(SSM_GROUPS, 1, hg), per_lane(ssm_d[j]), ssm_norm[j], bsz, seq)
            w_o = ssm_w_out
        xf, h = _mm_resid(a, w_o.astype(BF16), j, xf, mods[i, 0], norm_post[i, 0], (mods[i, 1], norm_pre[i, 1]), seq)
        u, w_down16 = _mm_w(h, mlp_up, i, epilogue=_relu2, cast_stack=mlp_down)
        nxt = (mods[i + 1, 0], norm_pre[i + 1, 0]) if i + 1 < depth else None
        xf, h = _mm_resid(u, w_down16[None], 0, xf, mods[i, 1], norm_post[i, 1], nxt, seq)
    return xf.reshape(bsz, seq, d)
```

```python
import functools
import math

import jax
import jax.numpy as jnp
from jax import lax
from jax.experimental import pallas as pl
from jax.experimental.pallas import tpu as pltpu

F32 = jnp.float32
BF16 = jnp.bfloat16

EPS = 1e-6
CHUNK = 64
N_MIXERS = 3
MLA_HEADS = 16
MLA_LORA = 512
MLA_NOPE = 128
MLA_ROPE = 64
MLA_V = 128
ROPE_THETA = 10000.0
SSM_HEAD_DIM = 64
SSM_GROUPS = 8
SSM_STATE = 128

LANES = 128
SUBLANES = 8
V7X_VMEM_BYTES = 64 * 1024 * 1024
HEAD_PAD = 2 * LANES


def _vmem_limit(block_bytes, scratch_bytes=0, temp_bytes=0):
    need = 2 * sum(block_bytes) + scratch_bytes + temp_bytes + (4 << 20)
    return int(min(need, V7X_VMEM_BYTES - (6 << 20)))


def _nbytes(shape, dtype):
    return math.prod(shape) * jnp.dtype(dtype).itemsize


def _params(sem, limit):
    return pltpu.CompilerParams(dimension_semantics=sem, vmem_limit_bytes=limit)


def _rms(x):
    return x * lax.rsqrt(jnp.mean(x * x, axis=-1, keepdims=True) + EPS)


def _modulated_norm(x, mod_ref, g_ref):
    h = _rms(x) * g_ref[...]
    return h * (1.0 + mod_ref[1:2, :]) + mod_ref[0:1, :]


def _silu(x):
    return x * (1.0 / (1.0 + jnp.exp(-x)))


def _ada_kernel(c_ref, w_ref, b_ref, o_ref):
    c_act = _silu(c_ref[...]).astype(BF16)
    y = jnp.dot(c_act, w_ref[...].astype(BF16), preferred_element_type=F32)
    o_ref[...] = y + b_ref[...]


def _ada_all(c, ada_w, ada_b):
    nsub = ada_w.shape[0] * ada_w.shape[1]
    bsz, d = c.shape
    n = ada_w.shape[-1]
    w = ada_w.reshape(nsub, d, n)
    b = ada_b.reshape(nsub, 1, n)
    tn = 1024
    blocks = [_nbytes((d, tn), F32), _nbytes((bsz, tn), F32)]
    return pl.pallas_call(
        _ada_kernel,
        out_shape=jax.ShapeDtypeStruct((nsub, bsz, n), F32),
        grid=(nsub, n // tn),
        in_specs=[
            pl.BlockSpec((bsz, d), lambda s, j: (0, 0)),
            pl.BlockSpec((None, d, tn), lambda s, j: (s, 0, j)),
            pl.BlockSpec((None, 1, tn), lambda s, j: (s, 0, j)),
        ],
        out_specs=pl.BlockSpec((None, bsz, tn), lambda s, j: (s, 0, j)),
        compiler_params=_params(("arbitrary", "arbitrary"), _vmem_limit(blocks, temp_bytes=_nbytes((d, tn), BF16))),
        name="ada_mod",
    )(c, w, b)


def _rope_kernel(pos_ref, freq_ref, cos_ref, sin_ref):
    ang = pos_ref[...] * freq_ref[...]
    lane = lax.broadcasted_iota(jnp.int32, ang.shape, 1)
    half = MLA_ROPE // 2
    cos_ref[...] = jnp.where(lane < MLA_ROPE, jnp.cos(ang), 0.0)
    s = jnp.sin(ang)
    sin_ref[...] = jnp.where(lane < half, -s, jnp.where(lane < MLA_ROPE, s, 0.0))


def _rope_tables(positions):
    t = positions.size
    half = MLA_ROPE // 2
    inv_freq = ROPE_THETA ** (-jnp.arange(0, MLA_ROPE, 2, dtype=F32) / MLA_ROPE)
    freq = jnp.concatenate([inv_freq, inv_freq, jnp.zeros((LANES - 2 * half,), F32)]).reshape(1, LANES)
    pos = positions.astype(F32).reshape(t, 1)
    tm = min(t, 1024)
    out = jax.ShapeDtypeStruct((t, LANES), F32)
    return pl.pallas_call(
        _rope_kernel,
        out_shape=(out, out),
        grid=(t // tm,),
        in_specs=[pl.BlockSpec((tm, 1), lambda i: (i, 0)), pl.BlockSpec((1, LANES), lambda i: (0, 0))],
        out_specs=(pl.BlockSpec((tm, LANES), lambda i: (i, 0)), pl.BlockSpec((tm, LANES), lambda i: (i, 0))),
        compiler_params=_params(("arbitrary",), _vmem_limit([_nbytes((tm, LANES), F32)] * 3, temp_bytes=8 << 20)),
        name="rope_tables",
    )(pos, freq)


def _rope_slab(z, cos_ref, sin_ref):
    half = MLA_ROPE // 2
    lane = lax.broadcasted_iota(jnp.int32, z.shape, 1)
    partner = jnp.where(lane < half, pltpu.roll(z, LANES - half, axis=1), pltpu.roll(z, half, axis=1))
    return z * cos_ref[...] + partner * sin_ref[...]


def _relu2(y):
    r = jnp.maximum(y, 0.0)
    return r * r


def _norm_mod_kernel(x_ref, mod_ref, g_ref, h_ref):
    h_ref[...] = _modulated_norm(x_ref[...], mod_ref, g_ref).astype(BF16)


def _norm_mod(x, mod, g, seq, *, tm=1024):
    t, d = x.shape
    tiles_per_batch = seq // tm
    blocks = [_nbytes((tm, d), F32), _nbytes((tm, d), BF16)]
    return pl.pallas_call(
        _norm_mod_kernel,
        out_shape=jax.ShapeDtypeStruct((t, d), BF16),
        grid=(t // tm,),
        in_specs=[
            pl.BlockSpec((tm, d), lambda i: (i, 0)),
            pl.BlockSpec((None, 3, d), lambda i: (i // tiles_per_batch, 0, 0)),
            pl.BlockSpec((1, d), lambda i: (0, 0)),
        ],
        out_specs=pl.BlockSpec((tm, d), lambda i: (i, 0)),
        compiler_params=_params(("arbitrary",), _vmem_limit(blocks, temp_bytes=2 * _nbytes((tm, d), F32))),
        name="norm_mod",
    )(x, mod, g.reshape(1, d))


def _mm_w_kernel(*refs, epilogue, side_cast):
    if side_cast:
        h_ref, w_ref, c_ref, o_ref, c16_ref, w_scr = refs
        c16_ref[...] = c_ref[...].astype(BF16)
    else:
        h_ref, w_ref, o_ref, w_scr = refs

    @pl.when(pl.program_id(1) == 0)
    def _():
        w_scr[...] = w_ref[...].astype(BF16)

    y = jnp.dot(h_ref[...], w_scr[...], preferred_element_type=F32)
    o_ref[...] = epilogue(y).astype(o_ref.dtype)


def _identity(y):
    return y


def _mm_w(h, w_stack, layer, *, epilogue, ncols=None, cast_stack=None, tm=1024, tn=1024):
    t, d = h.shape
    n = w_stack.shape[2] if ncols is None else ncols
    nj, ni = n // tn, t // tm
    blocks = [_nbytes((tm, d), BF16), _nbytes((d, tn), F32), _nbytes((tm, tn), BF16)]
    in_specs = [
        pl.BlockSpec((tm, d), lambda j, i: (i, 0)),
        pl.BlockSpec((None, d, tn), lambda j, i: (layer, 0, j)),
    ]
    args = [h, w_stack]
    out_shape = [jax.ShapeDtypeStruct((t, n), BF16)]
    out_specs = [pl.BlockSpec((tm, tn), lambda j, i: (i, j))]
    if cast_stack is not None:
        rows, cols = cast_stack.shape[1:]
        assert rows % (nj * ni * 2 * SUBLANES) == 0, "side-cast slabs must tile the weight rows exactly"
        slab = rows // (nj * ni)
        in_specs.append(pl.BlockSpec((None, slab, cols), lambda j, i: (layer, j * ni + i, 0)))
        args.append(cast_stack)
        out_shape.append(jax.ShapeDtypeStruct((rows, cols), BF16))
        out_specs.append(pl.BlockSpec((slab, cols), lambda j, i: (j * ni + i, 0)))
        blocks += [_nbytes((slab, cols), F32), _nbytes((slab, cols), BF16)]
    outs = pl.pallas_call(
        functools.partial(_mm_w_kernel, epilogue=epilogue, side_cast=cast_stack is not None),
        out_shape=tuple(out_shape),
        grid=(nj, ni),
        in_specs=in_specs,
        out_specs=tuple(out_specs),
        scratch_shapes=[pltpu.VMEM((d, tn), BF16)],
        compiler_params=_params(("arbitrary", "arbitrary"),
                                _vmem_limit(blocks, _nbytes((d, tn), BF16), 2 * _nbytes((tm, tn), F32))),
        name="mm_w",
    )(*args)
    return outs if cast_stack is not None else outs[0]


_EPILOGUE_ROWS = 256


def _mm_resid_kernel(*refs, nk, emit_next):
    a_ref, w_ref, x_ref, mod_ref, g_ref = refs[:5]
    refs = refs[5:]
    if emit_next:
        modn_ref, gn_ref, o_ref, hn_ref = refs[:4]
        refs = refs[4:]
    else:
        o_ref = refs[0]
        refs = refs[1:]
    i, k = pl.program_id(0), pl.program_id(1)
    tm = o_ref.shape[0]

    if nk > 1:
        x_scr, x_sem = refs

        def x_copy():
            return pltpu.make_async_copy(x_ref.at[pl.ds(pl.multiple_of(i * tm, tm), tm), :], x_scr, x_sem)

        def partial_product(rows):
            return jnp.dot(a_ref[rows, :], w_ref[...], preferred_element_type=F32)

        @pl.when(k == 0)
        def _():
            x_copy().start()
            for r0 in range(0, tm, _EPILOGUE_ROWS):
                rows = slice(r0, r0 + _EPILOGUE_ROWS)
                o_ref[rows, :] = partial_product(rows)

        if nk > 2:
            @pl.when((k > 0) & (k < nk - 1))
            def _():
                for r0 in range(0, tm, _EPILOGUE_ROWS):
                    rows = slice(r0, r0 + _EPILOGUE_ROWS)
                    o_ref[rows, :] += partial_product(rows)
    else:
        x_scr = x_ref

    def finish():
        gate_g = mod_ref[2:3, :] * g_ref[...]
        if emit_next:
            scale_g = gn_ref[...] * (1.0 + modn_ref[1:2, :])
        for r0 in range(0, tm, _EPILOGUE_ROWS):
            rows = slice(r0, r0 + _EPILOGUE_ROWS)
            y = jnp.dot(a_ref[rows, :], w_ref[...], preferred_element_type=F32)
            if nk > 1:
                y = o_ref[rows, :] + y
            x_new = x_scr[rows, :] + _rms(y) * gate_g
            o_ref[rows, :] = x_new
            if emit_next:
                hn_ref[rows, :] = (_rms(x_new) * scale_g + modn_ref[0:1, :]).astype(BF16)

    if nk > 1:
        @pl.when(k == nk - 1)
        def _():
            x_copy().wait()
            finish()
    else:
        finish()


def _mm_resid(a, w_stack, layer, x, mod, g, nxt, seq):
    t, kdim = a.shape
    d = w_stack.shape[2]
    tm, tk = 512, min(kdim, 2048)
    nk = kdim // tk
    tiles_per_batch = seq // tm
    emit_next = nxt is not None
    batch_map = lambda i, k: (i // tiles_per_batch, 0, 0)
    fixed = lambda i, k: (0, 0)
    x_spec = pl.BlockSpec((tm, d), lambda i, k: (i, 0)) if nk == 1 else pl.BlockSpec(memory_space=pl.ANY)
    in_specs = [
        pl.BlockSpec((tm, tk), lambda i, k: (i, k)),
        pl.BlockSpec((None, tk, d), lambda i, k: (layer, k, 0)),
        x_spec,
        pl.BlockSpec((None, 3, d), batch_map),
        pl.BlockSpec((1, d), fixed),
    ]
    args = [a, w_stack, x, mod, g.reshape(1, d)]
    out_shape = [jax.ShapeDtypeStruct((t, d), F32)]
    out_specs = [pl.BlockSpec((tm, d), lambda i, k: (i, 0))]
    blocks = [_nbytes((tm, tk), BF16), _nbytes((tk, d), BF16), _nbytes((tm, d), F32)]
    if emit_next:
        in_specs += [pl.BlockSpec((None, 3, d), batch_map), pl.BlockSpec((1, d), fixed)]
        args += [nxt[0], nxt[1].reshape(1, d)]
        out_shape.append(jax.ShapeDtypeStruct((t, d), BF16))
        out_specs.append(pl.BlockSpec((tm, d), lambda i, k: (i, 0)))
        blocks.append(_nbytes((tm, d), BF16))
    if nk == 1:
        scratch, scratch_bytes = [], 0
        blocks.append(_nbytes((tm, d), F32))
    else:
        scratch, scratch_bytes = [pltpu.VMEM((tm, d), F32), pltpu.SemaphoreType.DMA(())], _nbytes((tm, d), F32)
    outs = pl.pallas_call(
        functools.partial(_mm_resid_kernel, nk=nk, emit_next=emit_next),
        out_shape=tuple(out_shape),
        grid=(t // tm, nk),
        in_specs=in_specs,
        out_specs=tuple(out_specs),
        scratch_shapes=scratch,
        compiler_params=_params(("arbitrary", "arbitrary"),
                                _vmem_limit(blocks, scratch_bytes, 8 * _nbytes((_EPILOGUE_ROWS, d), F32))),
        name="mm_resid",
    )(*args)
    return (outs[0], outs[1]) if emit_next else (outs[0], None)


def _mla_down_kernel(h_ref, w_ref, qn_ref, kvn_ref, cos_ref, sin_ref, cq_ref, ckv_ref, kr_ref, w_scr):
    @pl.when(pl.program_id(0) == 0)
    def _():
        w_scr[...] = w_ref[...].astype(BF16)

    y = jnp.dot(h_ref[...], w_scr[...], preferred_element_type=F32)
    cq_ref[...] = (_rms(y[:, :MLA_LORA]) * qn_ref[...]).astype(BF16)
    ckv_ref[...] = (_rms(y[:, MLA_LORA:2 * MLA_LORA]) * kvn_ref[...]).astype(BF16)
    slab = jnp.concatenate([y[:, 2 * MLA_LORA:], jnp.zeros((y.shape[0], LANES - MLA_ROPE), F32)], axis=1)
    kr_ref[...] = _rope_slab(slab, cos_ref, sin_ref).astype(BF16)


def _mla_down(h, w_stack, layer, q_norm, kv_norm, cos_t, sin_t, *, tm=1024):
    t, d = h.shape
    n = w_stack.shape[2]
    blocks = [_nbytes((tm, d), BF16), _nbytes((d, n), F32), 3 * _nbytes((tm, MLA_LORA), BF16), 2 * _nbytes((tm, LANES), F32)]
    row = lambda i: (i, 0)
    fixed = lambda i: (0, 0)
    return pl.pallas_call(
        _mla_down_kernel,
        out_shape=(jax.ShapeDtypeStruct((t, MLA_LORA), BF16), jax.ShapeDtypeStruct((t, MLA_LORA), BF16),
                   jax.ShapeDtypeStruct((t, LANES), BF16)),
        grid=(t // tm,),
        in_specs=[
            pl.BlockSpec((tm, d), row),
            pl.BlockSpec((None, d, n), lambda i: (layer, 0, 0)),
            pl.BlockSpec((1, MLA_LORA), fixed),
            pl.BlockSpec((1, MLA_LORA), fixed),
            pl.BlockSpec((tm, LANES), row),
            pl.BlockSpec((tm, LANES), row),
        ],
        out_specs=(pl.BlockSpec((tm, MLA_LORA), row), pl.BlockSpec((tm, MLA_LORA), row), pl.BlockSpec((tm, LANES), row)),
        scratch_shapes=[pltpu.VMEM((d, n), BF16)],
        compiler_params=_params(("arbitrary",), _vmem_limit(blocks, _nbytes((d, n), BF16), 3 * _nbytes((tm, n), F32))),
        name="mla_down",
    )(h, w_stack, q_norm.reshape(1, -1), kv_norm.reshape(1, -1), cos_t, sin_t)


def _q_up_kernel(a_ref, w_ref, cos_ref, sin_ref, q_ref, *, heads, scale):
    y = jnp.dot(a_ref[...], w_ref[...], preferred_element_type=F32) * scale
    for h in range(heads):
        base = h * HEAD_PAD
        q_ref[h, :, :LANES] = y[:, base:base + LANES].astype(BF16)
        q_ref[h, :, LANES:] = _rope_slab(y[:, base + LANES:base + HEAD_PAD], cos_ref, sin_ref).astype(BF16)


def _kv_up_kernel(a_ref, w_ref, kr_ref, k_ref, vt_ref, *, heads):
    y = jnp.dot(a_ref[...], w_ref[...], preferred_element_type=F32)
    kr = kr_ref[...]
    for h in range(heads):
        base = h * (MLA_NOPE + MLA_V)
        k_ref[h, :, :LANES] = y[:, base:base + MLA_NOPE].astype(BF16)
        k_ref[h, :, LANES:] = kr
        vt_ref[h] = y[:, base + MLA_NOPE:base + MLA_NOPE + MLA_V].T.astype(BF16)


def _q_up(cq, w_uq, cos_t, sin_t, bsz, seq, *, tm=1024, heads=8):
    t, r = cq.shape
    tiles_per_batch = seq // tm
    scale = math.log2(math.e) / math.sqrt(MLA_NOPE + MLA_ROPE)
    tn = heads * HEAD_PAD
    blocks = [_nbytes((tm, r), BF16), _nbytes((r, tn), BF16), 2 * _nbytes((tm, LANES), F32), _nbytes((heads, tm, HEAD_PAD), BF16)]
    return pl.pallas_call(
        functools.partial(_q_up_kernel, heads=heads, scale=scale),
        out_shape=jax.ShapeDtypeStruct((bsz, MLA_HEADS, seq, HEAD_PAD), BF16),
        grid=(t // tm, MLA_HEADS // heads),
        in_specs=[
            pl.BlockSpec((tm, r), lambda i, j: (i, 0)),
            pl.BlockSpec((r, tn), lambda i, j: (0, j)),
            pl.BlockSpec((tm, LANES), lambda i, j: (i, 0)),
            pl.BlockSpec((tm, LANES), lambda i, j: (i, 0)),
        ],
        out_specs=pl.BlockSpec((None, heads, tm, HEAD_PAD),
                               lambda i, j: (i // tiles_per_batch, j, i % tiles_per_batch, 0)),
        compiler_params=_params(("arbitrary", "arbitrary"), _vmem_limit(blocks, temp_bytes=2 * _nbytes((tm, tn), F32))),
        name="mla_q_up",
    )(cq, w_uq, cos_t, sin_t)


def _kv_up(ckv, w_ukv, kr, bsz, seq, *, tm=1024, heads=8):
    t, r = ckv.shape
    tiles_per_batch = seq // tm
    tn = heads * (MLA_NOPE + MLA_V)
    blocks = [_nbytes((tm, r), BF16), _nbytes((r, tn), BF16), _nbytes((tm, LANES), BF16),
              _nbytes((heads, tm, HEAD_PAD), BF16), _nbytes((heads, tm, MLA_V), BF16)]
    out_map = lambda i, j: (i // tiles_per_batch, j, i % tiles_per_batch, 0)
    vt_map = lambda i, j: (i // tiles_per_batch, j, 0, i % tiles_per_batch)
    return pl.pallas_call(
        functools.partial(_kv_up_kernel, heads=heads),
        out_shape=(jax.ShapeDtypeStruct((bsz, MLA_HEADS, seq, HEAD_PAD), BF16),
                   jax.ShapeDtypeStruct((bsz, MLA_HEADS, MLA_V, seq), BF16)),
        grid=(t // tm, MLA_HEADS // heads),
        in_specs=[
            pl.BlockSpec((tm, r), lambda i, j: (i, 0)),
            pl.BlockSpec((r, tn), lambda i, j: (0, j)),
            pl.BlockSpec((tm, LANES), lambda i, j: (i, 0)),
        ],
        out_specs=(pl.BlockSpec((None, heads, tm, HEAD_PAD), out_map), pl.BlockSpec((None, heads, MLA_V, tm), vt_map)),
        compiler_params=_params(("arbitrary", "arbitrary"), _vmem_limit(blocks, temp_bytes=2 * _nbytes((tm, tn), F32))),
        name="mla_kv_up",
    )(ckv, w_ukv, kr)


def _attn_kernel(q_ref, k_ref, vt_ref, o_ref, *scratch, tq, tk, lanes_q):
    def tile(qi, carry):
        _attn_tile(qi, q_ref, k_ref, vt_ref, o_ref, *scratch, tq=tq, tk=tk, lanes_q=lanes_q)
        return carry

    lax.fori_loop(0, q_ref.shape[0] // tq, tile, 0)


def _attn_tile(qi, q_ref, k_ref, vt_ref, o_ref, m_scr, l_scr, acc_scr, sa_scr, sb_scr, bma_scr, bmb_scr, *, tq, tk, lanes_q):
    ngrp = tq // lanes_q

    def q_rows(c):
        return pl.ds(pl.multiple_of(qi * tq + c * lanes_q, lanes_q), lanes_q)

    ndiag = tq // tk
    assert ndiag == 2, "the two-slot score ring below assumes two key blocks per query tile"
    nfull = qi * ndiag
    ring = ((sa_scr, bma_scr), (sb_scr, bmb_scr))
    m_scr[...] = jnp.full(m_scr.shape, -jnp.inf, F32)
    l_scr[...] = jnp.zeros(l_scr.shape, F32)
    acc_scr[...] = jnp.zeros(acc_scr.shape, F32)

    def skipped(c, diag):
        return diag is not None and (c + 1) * lanes_q <= diag * tk

    def scores(blk, diag, slot):
        s_scr, bm_scr = ring[slot]
        kb = k_ref[pl.ds(pl.multiple_of(blk * tk, tk), tk), :]
        for c in range(ngrp):
            if skipped(c, diag):
                continue
            qc = q_ref[q_rows(c), :]
            s = lax.dot_general(kb, qc, (((1,), (1,)), ((), ())), preferred_element_type=F32)
            if diag is not None and c * lanes_q < (diag + 1) * tk:
                k_chunk = (lax.broadcasted_iota(jnp.int32, s.shape, 0) + diag * tk) // CHUNK
                q_chunk = (lax.broadcasted_iota(jnp.int32, s.shape, 1) + c * lanes_q) // CHUNK
                s = jnp.where(k_chunk <= q_chunk, s, -jnp.inf)
            s_scr[c] = s
            bm_scr[c] = jnp.max(s, axis=0, keepdims=True)

    def accumulate(blk, diag, slot):
        s_scr, bm_scr = ring[slot]
        vtb = vt_ref[:, pl.ds(pl.multiple_of(blk * tk, tk), tk)]
        for c in range(ngrp):
            if skipped(c, diag):
                continue
            m_prev = m_scr[c]
            m_new = jnp.maximum(m_prev, bm_scr[c])
            alpha = jnp.exp2(m_prev - m_new)
            p = jnp.exp2(s_scr[c] - m_new)
            l_scr[c] = alpha * l_scr[c] + jnp.sum(p, axis=0, keepdims=True)
            acc_scr[c] = alpha * acc_scr[c] + jnp.dot(vtb, p.astype(BF16), preferred_element_type=F32)
            m_scr[c] = m_new

    @pl.when(qi > 0)
    def _():
        scores(0, None, 0)

    @pl.when(qi == 0)
    def _():
        scores(0, 0, 0)

    def body(i, carry):
        b0 = 2 * i
        scores(b0 + 1, None, 1)
        accumulate(b0, None, 0)
        scores(b0 + 2, None, 0)
        accumulate(b0 + 1, None, 1)
        return carry

    lax.fori_loop(0, qi - 1, body, 0)

    @pl.when(qi > 0)
    def _():
        scores(nfull - 1, None, 1)
        accumulate(nfull - 2, None, 0)
        scores(nfull, 0, 0)
        accumulate(nfull - 1, None, 1)

    scores(nfull + 1, 1, 1)
    accumulate(nfull, 0, 0)
    accumulate(nfull + 1, 1, 1)
    for c in range(ngrp):
        o_ref[q_rows(c), :] = (acc_scr[c] / l_scr[c]).T.astype(o_ref.dtype)


def _attention(q, k, vt, *, tq=1024, tk=512, lanes_q=256):
    bsz, heads, seq, _ = q.shape
    ngrp = tq // lanes_q
    blocks = [_nbytes((seq, HEAD_PAD), BF16), _nbytes((seq, HEAD_PAD), BF16), _nbytes((MLA_V, seq), BF16), _nbytes((seq, MLA_V), BF16)]
    scratch = (2 * _nbytes((ngrp, SUBLANES, lanes_q), F32) + _nbytes((ngrp, MLA_V, lanes_q), F32)
               + 2 * _nbytes((ngrp, tk, lanes_q), F32) + 2 * _nbytes((ngrp, SUBLANES, lanes_q), F32))
    return pl.pallas_call(
        functools.partial(_attn_kernel, tq=tq, tk=tk, lanes_q=lanes_q),
        out_shape=jax.ShapeDtypeStruct((bsz * seq, heads * MLA_V), BF16),
        grid=(bsz, heads),
        in_specs=[
            pl.BlockSpec((None, None, seq, HEAD_PAD), lambda b, h: (b, h, 0, 0)),
            pl.BlockSpec((None, None, seq, HEAD_PAD), lambda b, h: (b, h, 0, 0)),
            pl.BlockSpec((None, None, MLA_V, seq), lambda b, h: (b, h, 0, 0)),
        ],
        out_specs=pl.BlockSpec((seq, MLA_V), lambda b, h: (b, h)),
        scratch_shapes=[pltpu.VMEM((ngrp, 1, lanes_q), F32), pltpu.VMEM((ngrp, 1, lanes_q), F32),
                        pltpu.VMEM((ngrp, MLA_V, lanes_q), F32),
                        pltpu.VMEM((ngrp, tk, lanes_q), F32), pltpu.VMEM((ngrp, tk, lanes_q), F32),
                        pltpu.VMEM((ngrp, 1, lanes_q), F32), pltpu.VMEM((ngrp, 1, lanes_q), F32)],
        compiler_params=_params(("arbitrary", "arbitrary"),
                                _vmem_limit(blocks, scratch, 8 * _nbytes((tk, tq), F32))),
        name="mla_attention",
    )(q, k, vt)


def _causal_conv(t, tail_ref, w_ref, first_of_batch):
    kw = w_ref.shape[0]
    tm = t.shape[0]
    hist = jnp.where(first_of_batch, 0.0, tail_ref[...])
    tail_ref[...] = t[tm - SUBLANES:, :]
    row = lax.broadcasted_iota(jnp.int32, hist.shape, 0)
    acc = t * w_ref[kw - 1:kw, :]
    for k in range(kw - 1):
        s = kw - 1 - k
        rolled = pltpu.roll(t, s, axis=0)
        head = jnp.where(row < s, pltpu.roll(hist, s, axis=0), rolled[:SUBLANES])
        acc = acc + jnp.concatenate([head, rolled[SUBLANES:]], axis=0) * w_ref[k:k + 1, :]
    return acc


def _conv_in_kernel(h_ref, wb_ref, wc_ref, wu_ref, cw_ref, o_ref, wb_scr, wc_scr, wu_scr, tail_scr, *, tiles_per_batch):
    i = pl.program_id(1)

    @pl.when(i == 0)
    def _():
        wb_scr[...] = wb_ref[...].astype(BF16)
        wc_scr[...] = wc_ref[...].astype(BF16)
        wu_scr[...] = wu_ref[...].astype(BF16)

    h = h_ref[...]
    gc = jnp.dot(h, wc_scr[...], preferred_element_type=F32)
    u = jnp.dot(h, wu_scr[...], preferred_element_type=F32)
    conv = _causal_conv(gc * u, tail_scr, cw_ref, i % tiles_per_batch == 0)
    gb = jnp.dot(h, wb_scr[...], preferred_element_type=F32)
    o_ref[...] = (gb * conv).astype(o_ref.dtype)


def _conv_in(h, w_stack, layer, conv_w, seq, *, tm=512, tn=512):
    t, d = h.shape
    n = w_stack.shape[2] // 3
    nj = n // tn
    tiles_per_batch = seq // tm
    kw = conv_w.shape[0]
    blocks = [_nbytes((tm, d), BF16), 3 * _nbytes((d, tn), F32), _nbytes((tm, tn), BF16)]
    scratch = 3 * _nbytes((d, tn), BF16) + _nbytes((SUBLANES, tn), F32)
    wspec = lambda off: pl.BlockSpec((None, d, tn), lambda j, i: (layer, 0, j + off * nj))
    return pl.pallas_call(
        functools.partial(_conv_in_kernel, tiles_per_batch=tiles_per_batch),
        out_shape=jax.ShapeDtypeStruct((t, n), BF16),
        grid=(nj, t // tm),
        in_specs=[
            pl.BlockSpec((tm, d), lambda j, i: (i, 0)),
            wspec(0), wspec(1), wspec(2),
            pl.BlockSpec((kw, tn), lambda j, i: (0, j)),
        ],
        out_specs=pl.BlockSpec((tm, tn), lambda j, i: (i, j)),
        scratch_shapes=[pltpu.VMEM((d, tn), BF16), pltpu.VMEM((d, tn), BF16), pltpu.VMEM((d, tn), BF16),
                        pltpu.VMEM((SUBLANES, tn), F32)],
        compiler_params=_params(("arbitrary", "arbitrary"), _vmem_limit(blocks, scratch, 6 * _nbytes((tm, tn), F32))),
        name="conv_in",
    )(h, w_stack, w_stack, w_stack, conv_w)


def _softplus(x):
    return jnp.maximum(x, 0.0) + jnp.log1p(jnp.exp(-jnp.abs(x)))


def _ssd_xbc_kernel(h_ref, w_ref, cw_ref, cb_ref, o_ref, w_scr, tail_scr, y_scr, *, tiles_per_batch):
    i = pl.program_id(1)

    @pl.when(i == 0)
    def _():
        w_scr[...] = w_ref[...].astype(BF16)
        y_scr[...] = jnp.zeros(y_scr.shape, F32)
        tail_scr[...] = jnp.zeros(tail_scr.shape, F32)

    conv = _causal_conv(y_scr[...], tail_scr, cw_ref, (i - 1) % tiles_per_batch == 0) + cb_ref[...]
    o_ref[...] = _silu(conv).astype(o_ref.dtype)
    y_scr[...] = jnp.dot(h_ref[...], w_scr[...], preferred_element_type=F32)


def _ssd_xbc(h, w_stack, layer, conv_w, conv_b, col0, seq, *, tm=512, tn=1024):
    t, d = h.shape
    kw, n = conv_w.shape
    ni = t // tm
    tiles_per_batch = seq // tm
    joff = col0 // tn
    blocks = [_nbytes((tm, d), BF16), _nbytes((d, tn), F32), _nbytes((tm, tn), BF16)]
    scratch = _nbytes((d, tn), BF16) + _nbytes((SUBLANES, tn), F32) + _nbytes((tm, tn), F32)
    return pl.pallas_call(
        functools.partial(_ssd_xbc_kernel, tiles_per_batch=tiles_per_batch),
        out_shape=jax.ShapeDtypeStruct((t, n), BF16),
        grid=(n // tn, ni + 1),
        in_specs=[
            pl.BlockSpec((tm, d), lambda j, i: (jnp.minimum(i, ni - 1), 0)),
            pl.BlockSpec((None, d, tn), lambda j, i: (layer, 0, j + joff)),
            pl.BlockSpec((kw, tn), lambda j, i: (0, j)),
            pl.BlockSpec((1, tn), lambda j, i: (0, j)),
        ],
        out_specs=pl.BlockSpec((tm, tn), lambda j, i: (jnp.maximum(i - 1, 0), j)),
        scratch_shapes=[pltpu.VMEM((d, tn), BF16), pltpu.VMEM((SUBLANES, tn), F32), pltpu.VMEM((tm, tn), F32)],
        compiler_params=_params(("arbitrary", "arbitrary"), _vmem_limit(blocks, scratch, 6 * _nbytes((tm, tn), F32))),
        name="ssd_xbc",
    )(h, w_stack, conv_w, conv_b.reshape(1, -1))


def _ssd_dt_kernel(h_ref, w_ref, b_ref, dt_ref):
    dt_ref[...] = _softplus(jnp.dot(h_ref[...], w_ref[...], preferred_element_type=F32) + b_ref[...])


def _ssd_dt(h, w_dt, dt_bias, *, tm=1024):
    t, d = h.shape
    nh = w_dt.shape[1]
    blocks = [_nbytes((tm, d), BF16), _nbytes((d, LANES), BF16), _nbytes((tm, LANES), F32)]
    return pl.pallas_call(
        _ssd_dt_kernel,
        out_shape=jax.ShapeDtypeStruct((t, nh), F32),
        grid=(t // tm,),
        in_specs=[
            pl.BlockSpec((tm, d), lambda i: (i, 0)),
            pl.BlockSpec((d, nh), lambda i: (0, 0)),
            pl.BlockSpec((1, nh), lambda i: (0, 0)),
        ],
        out_specs=pl.BlockSpec((tm, nh), lambda i: (i, 0)),
        compiler_params=_params(("arbitrary",), _vmem_limit(blocks, temp_bytes=4 * _nbytes((tm, LANES), F32))),
        name="ssd_dt",
    )(h, w_dt, dt_bias.reshape(1, nh))


def _split3(x):
    hi = x.astype(BF16)
    r1 = x - hi.astype(F32)
    mid = r1.astype(BF16)
    lo = (r1 - mid.astype(F32)).astype(BF16)
    return hi, mid, lo


def _dot01(sel, x, *, sel_left):
    out = None
    for part in _split3(x):
        d = (jnp.dot(sel, part, preferred_element_type=F32) if sel_left
             else jnp.dot(part, sel, preferred_element_type=F32))
        out = d if out is None else out + d
    return out


def _ssd_kernel(z_ref, x_ref, b_ref, c_ref, dt_ref, alog_ref, dskip_ref, nw_ref, o_ref, state_scr, *, tt):
    hg = dt_ref.shape[-1]
    gw = x_ref.shape[-1]
    hd = gw // hg
    nch = tt // CHUNK

    @pl.when(pl.program_id(2) == 0)
    def _():
        state_scr[...] = jnp.zeros(state_scr.shape, F32)

    def iota(shape, axis):
        return lax.broadcasted_iota(jnp.int32, shape, axis)

    expand = (iota((hg, gw), 1) // hd == iota((hg, gw), 0)).astype(BF16)
    r2, c2 = iota((tt, tt), 0), iota((tt, tt), 1)
    tri = ((c2 <= r2) & (c2 // CHUNK == r2 // CHUNK)).astype(BF16)
    lrow, lcol = iota((CHUNK, gw), 0), iota((CHUNK, gw), 1) % hd
    half = gw // 2
    bd_keep = iota((half * CHUNK // hd, half), 0) // CHUNK == iota((half * CHUNK // hd, half), 1) // hd

    a_head = -jnp.exp(alog_ref[...]) * math.log2(math.e)
    dt_head = dt_ref[...]
    acum_head = _dot01(tri, dt_head * a_head, sel_left=True)
    dt_all = _dot01(expand, dt_head, sel_left=False)
    acum_all = _dot01(expand, acum_head, sel_left=False)

    for ci in range(nch):
        rows = slice(ci * CHUNK, (ci + 1) * CHUNK)
        x = x_ref[rows, :].astype(F32)
        bm = b_ref[rows, :]
        cm = c_ref[rows, :]
        dt, acum = dt_all[rows], acum_all[rows]
        r = jnp.sum(jnp.where(lrow == lcol, acum, 0.0), axis=0, keepdims=True)
        decay = jnp.exp2(jnp.where(lrow >= lcol, acum - r, -jnp.inf))
        cb = lax.dot_general(cm, bm, (((1,), (1,)), ((), ())), preferred_element_type=F32)
        m = (jnp.concatenate([cb] * hg, axis=1) * decay).astype(BF16)
        xdt = x * dt
        xdt16 = xdt.astype(BF16)
        ydiag = []
        for hf in range(2):
            cols = slice(hf * half, (hf + 1) * half)
            blockdiag = jnp.where(bd_keep, jnp.concatenate([xdt16[:, cols]] * (half // hd), axis=0), jnp.zeros((), BF16))
            ydiag.append(jnp.dot(m[:, cols], blockdiag, preferred_element_type=F32))
        last = acum[CHUNK - 1:CHUNK, :]
        xw = (xdt * jnp.exp2(last - acum)).astype(BF16)
        st = lax.dot_general(bm, xw, (((0,), (0,)), ((), ())), preferred_element_type=F32)
        prev = state_scr[...]
        yoff = jnp.dot(cm, prev.astype(BF16), preferred_element_type=F32) * jnp.exp2(acum)
        state_scr[...] = prev * jnp.exp2(last) + st
        y = jnp.concatenate(ydiag, axis=1) + yoff + x * dskip_ref[...]
        gated = y * _silu(z_ref[rows, :].astype(F32))
        o_ref[rows, :] = (_rms(gated) * nw_ref[...]).astype(o_ref.dtype)


def _ssd_scan(z, xbc, dt_g, a_log, d_skip, norm_w, bsz, seq, *, tt=512):
    t, d_inner = z.shape
    groups, _, hg = dt_g.shape
    gw = d_inner // groups
    n = SSM_STATE
    nt = seq // tt
    boff = d_inner // n
    coff = boff + groups
    row = lambda b, g, i: b * nt + i
    blocks = [2 * _nbytes((tt, gw), BF16), 2 * _nbytes((tt, n), BF16), _nbytes((tt, LANES), F32), _nbytes((tt, gw), BF16)]
    return pl.pallas_call(
        functools.partial(_ssd_kernel, tt=tt),
        out_shape=jax.ShapeDtypeStruct((t, d_inner), BF16),
        grid=(bsz, groups, nt),
        in_specs=[
            pl.BlockSpec((tt, gw), lambda b, g, i: (row(b, g, i), g)),
            pl.BlockSpec((tt, gw), lambda b, g, i: (row(b, g, i), g)),
            pl.BlockSpec((tt, n), lambda b, g, i: (row(b, g, i), boff + g)),
            pl.BlockSpec((tt, n), lambda b, g, i: (row(b, g, i), coff + g)),
            pl.BlockSpec((None, tt, hg), lambda b, g, i: (g, row(b, g, i), 0)),
            pl.BlockSpec((None, 1, hg), lambda b, g, i: (g, 0, 0)),
            pl.BlockSpec((None, 1, gw), lambda b, g, i: (g, 0, 0)),
            pl.BlockSpec((1, gw), lambda b, g, i: (0, g)),
        ],
        out_specs=pl.BlockSpec((tt, gw), lambda b, g, i: (row(b, g, i), g)),
        scratch_shapes=[pltpu.VMEM((n, gw), F32)],
        compiler_params=_params(("arbitrary", "arbitrary", "arbitrary"),
                                _vmem_limit(blocks, _nbytes((n, gw), F32), 24 * _nbytes((tt, gw), F32))),
        name="ssd_scan",
    )(z, xbc, xbc, xbc, dt_g, a_log, d_skip, norm_w.reshape(1, d_inner))


def _pad_heads(w, heads, width, padded):
    r = w.shape[0]
    w = w.reshape(r, heads, width)
    return jnp.pad(w, ((0, 0), (0, 0), (0, padded - width))).reshape(r, heads * padded)


def kernel(x, c, positions, ada_w, ada_b, norm_pre, norm_post, mla_w_down, mla_q_norm, mla_w_uq, mla_kv_norm, mla_w_ukv, mla_w_o, conv_w_in, conv_w, conv_w_out, ssm_w_in, ssm_conv_w, ssm_conv_b, ssm_dt_bias, ssm_a_log, ssm_d, ssm_norm, ssm_w_out, mlp_up, mlp_down):
    bsz, seq, d = x.shape
    depth = ada_w.shape[0]
    t = bsz * seq
    xf = x.reshape(t, d)

    mods = _ada_all(c, ada_w, ada_b).reshape(depth, 2, bsz, 3, d)
    cos_t, sin_t = _rope_tables(positions)

    h = _norm_mod(xf, mods[0, 0], norm_pre[0, 0], seq)
    for i in range(depth):
        kind, j = i % N_MIXERS, i // N_MIXERS
        if kind == 0:
            w_uq = _pad_heads(mla_w_uq[j], MLA_HEADS, MLA_NOPE + MLA_ROPE, HEAD_PAD).astype(BF16)
            cq, ckv, kr = _mla_down(h, mla_w_down, j, mla_q_norm[j], mla_kv_norm[j], cos_t, sin_t)
            q = _q_up(cq, w_uq, cos_t, sin_t, bsz, seq)
            k, vt = _kv_up(ckv, mla_w_ukv[j].astype(BF16), kr, bsz, seq)
            a = _attention(q, k, vt)
            w_o = mla_w_o
        elif kind == 1:
            a = _conv_in(h, conv_w_in, j, conv_w[j], seq)
            w_o = conv_w_out
        else:
            d_inner = ssm_w_out.shape[1]
            nh = ssm_dt_bias.shape[1]
            n_main = ssm_w_in.shape[2] - nh
            z = _mm_w(h, ssm_w_in, j, epilogue=_identity, ncols=d_inner)
            xbc = _ssd_xbc(h, ssm_w_in, j, ssm_conv_w[j], ssm_conv_b[j], d_inner, seq)
            dt = _ssd_dt(h, ssm_w_in[j, :, n_main:].astype(BF16), ssm_dt_bias[j])
            hg = nh // SSM_GROUPS
            gw = d_inner // SSM_GROUPS
            dt_g = dt.reshape(t, SSM_GROUPS, hg).transpose(1, 0, 2)
            per_lane = lambda p: jnp.repeat(p.reshape(SSM_GROUPS, hg), SSM_HEAD_DIM, axis=1).reshape(SSM_GROUPS, 1, gw)
            a = _ssd_scan(z, xbc, dt_g, ssm_a_log[j].reshape(SSM_GROUPS, 1, hg), per_lane(ssm_d[j]), ssm_norm[j], bsz, seq)
            w_o = ssm_w_out
        xf, h = _mm_resid(a, w_o.astype(BF16), j, xf, mods[i, 0], norm_post[i, 0], (mods[i, 1], norm_pre[i, 1]), seq)
        u, w_down16 = _mm_w(h, mlp_up, i, epilogue=_relu2, cast_stack=mlp_down)
        nxt = (mods[i + 1, 0], norm_pre[i + 1, 0]) if i + 1 < depth else None
        xf, h = _mm_resid(u, w_down16[None], 0, xf, mods[i, 1], norm_post[i, 1], nxt, seq)
    return xf.reshape(bsz, seq, d)
```
